```python
import jax, jax.numpy as jnp
from jax import lax
import numpy as np

D_MODEL = 1024
BATCH = 8
SEQ = 2048
DEPTH = 4

GRID_W = 64
CTX_LEN = 256
EPS = 1e-6
N_MOD = 6

RW_HEAD = 64
RW_HEADS = D_MODEL // RW_HEAD
RW_DIM = RW_HEADS * RW_HEAD
DECAY_LORA = 64
ICLR_LORA = 64
GATE_LORA = 128
RW_GN_EPS = 64e-5
RW_SPLIT_POINTS = (RW_DIM, 2 * RW_DIM, 3 * RW_DIM,
                   3 * RW_DIM + DECAY_LORA, 3 * RW_DIM + 2 * DECAY_LORA,
                   3 * RW_DIM + 2 * DECAY_LORA + ICLR_LORA, 3 * RW_DIM + 2 * DECAY_LORA + 2 * ICLR_LORA)
RWKV_COLS = 3 * RW_DIM + 2 * DECAY_LORA + 2 * ICLR_LORA + GATE_LORA

CONV_DIM = D_MODEL
CONV_WIDTH = 31
CONV_COLS = 2 * CONV_DIM

ATT_HEAD = 128
ATT_Q_HEADS = D_MODEL // ATT_HEAD
ATT_KV_HEADS = 2
ATT_GROUP = ATT_Q_HEADS // ATT_KV_HEADS
Q_COLS = ATT_Q_HEADS * ATT_HEAD
KV_COLS = ATT_KV_HEADS * ATT_HEAD
ATT_COLS = Q_COLS + 2 * KV_COLS
ROPE_THETA = 10000.0
Q_BLOCK = 128

N_BRANCH = 3
GATE_COLS = N_BRANCH * D_MODEL
IN_SPLIT_POINTS = (RWKV_COLS, RWKV_COLS + CONV_COLS, RWKV_COLS + CONV_COLS + ATT_COLS)
IN_COLS = RWKV_COLS + CONV_COLS + ATT_COLS + GATE_COLS

D_FF = 4 * D_MODEL

kernel_name = 'hybrid_rwkv7_conformer_gqa_dit_block'


def rms_norm(x, g):
    xf = x.astype(jnp.float32)
    y = xf * lax.rsqrt(jnp.mean(xf * xf, axis=-1, keepdims=True) + EPS)
    return (y * g.astype(jnp.float32)).astype(x.dtype)


def layer_norm(x, g, b):
    xf = x.astype(jnp.float32)
    mu = jnp.mean(xf, axis=-1, keepdims=True)
    var = jnp.mean(jnp.square(xf - mu), axis=-1, keepdims=True)
    y = (xf - mu) * lax.rsqrt(var + EPS)
    return (y * g.astype(jnp.float32) + b.astype(jnp.float32)).astype(x.dtype)


def modulate(x, g, shift, scale):
    return rms_norm(x, g) * (1 + scale) + shift


def centred_shift(z):
    zp = jnp.pad(z, ((0, 0), (1, 1), (0, 0)))
    return 0.5 * (zp[:, :-2] + zp[:, 2:])


def depthwise_conv(z, w, b):
    out = lax.conv_general_dilated(
        z, w[:, None, :], window_strides=(1,),
        padding=((CONV_WIDTH // 2, CONV_WIDTH // 2),),
        dimension_numbers=('NWC', 'WIO', 'NWC'),
        feature_group_count=z.shape[-1])
    return out + b


def axial_rope_tables(rows):
    row = jnp.repeat(jnp.arange(rows), GRID_W).astype(jnp.float32)
    col = jnp.tile(jnp.arange(GRID_W), rows).astype(jnp.float32)
    axis_dim = ATT_HEAD // 2
    freqs = ROPE_THETA ** (-jnp.arange(0, axis_dim, 2, dtype=jnp.float32) / axis_dim)
    ang = jnp.concatenate([row[:, None] * freqs, col[:, None] * freqs], axis=-1)
    return jnp.cos(ang), jnp.sin(ang)


def apply_rope(x, cos, sin):
    shape = (cos.shape[0],) + (1,) * (x.ndim - 3) + (cos.shape[1],)
    cos = cos.reshape(shape)
    sin = sin.reshape(shape)
    xf = x.astype(jnp.float32)
    x1, x2 = xf[..., 0::2], xf[..., 1::2]
    out = jnp.stack([x1 * cos - x2 * sin, x1 * sin + x2 * cos], axis=-1).reshape(x.shape)
    return out.astype(x.dtype)


def sdpa(q, k, v):
    s = jnp.einsum('bqhgd,bkhd->bhgqk', q, k).astype(jnp.float32) * (ATT_HEAD ** -0.5)
    p = jax.nn.softmax(s, axis=-1).astype(v.dtype)
    return jnp.einsum('bhgqk,bkhd->bqhgd', p, v)


def _rwkv_inputs(z, mu, w0, w2, a0, a2, g2, k_k, k_a):
    B, L, _ = z.shape
    z = z + mu * (centred_shift(z) - z)
    r, k, v, wd_f, wd_b, ad_f, ad_b, gd = jnp.split(z, RW_SPLIT_POINTS, axis=-1)

    def heads(t):
        return t.astype(jnp.float32).reshape(B, L, RW_HEADS, RW_HEAD)

    kk = heads(k * k_k)
    kk = kk / jnp.maximum(jnp.sqrt(jnp.sum(kk * kk, axis=-1, keepdims=True)), 1e-12)
    decays, keys, iclrs = [], [], []
    for d, (wd, ad) in enumerate(((wd_f, ad_f), (wd_b, ad_b))):
        w_raw = (w0[d] + jnp.tanh(wd) @ w2[d]).astype(jnp.float32)
        decays.append(heads(jnp.exp(-jnp.exp(-jax.nn.softplus(-w_raw) - 0.5))))
        a = jax.nn.sigmoid((a0[d] + ad @ a2[d]).astype(jnp.float32))
        keys.append(heads(k.astype(jnp.float32) * (1.0 + (a - 1.0) * k_a.astype(jnp.float32))))
        iclrs.append(heads(a))
    g = jax.nn.sigmoid(gd) @ g2
    return heads(r), heads(v), kk, g, decays, keys, iclrs


def _rwkv_scan(state0, r, decay, k, v, kk, a, reverse):
    seq = tuple(jnp.swapaxes(t, 0, 1) for t in (r, decay, k, v, kk, a))

    def step(S, inp):
        r_t, w_t, k_t, v_t, kk_t, a_t = inp
        s_kk = jnp.einsum('bhij,bhj->bhi', S, kk_t)
        S = (S * w_t[:, :, None, :]
             - s_kk[..., None] * (kk_t * a_t)[:, :, None, :]
             + v_t[..., None] * k_t[:, :, None, :])
        return S, jnp.einsum('bhij,bhj->bhi', S, r_t)

    S, y = lax.scan(step, state0, seq, reverse=reverse)
    return S, jnp.swapaxes(y, 0, 1)


def _rwkv_readout(y, bonus, g, ln_g, ln_b):
    B, L = y.shape[:2]
    mu = jnp.mean(y, axis=-1, keepdims=True)
    var = jnp.mean(jnp.square(y - mu), axis=-1, keepdims=True)
    yn = ((y - mu) * lax.rsqrt(var + RW_GN_EPS)).reshape(B, L, RW_DIM)
    yn = yn * ln_g.astype(jnp.float32) + ln_b.astype(jnp.float32)
    return ((yn + bonus.reshape(B, L, RW_DIM)) * g.astype(jnp.float32)).astype(g.dtype)


def rwkv_branch(zc, zl, mu, w0, w2, a0, a2, g2, k_k, k_a, r_k, ln_g, ln_b, ctx_out):
    prm = (mu, w0, w2, a0, a2, g2, k_k, k_a)
    rc, vc, kkc, gc, dec_c, key_c, icl_c = _rwkv_inputs(zc, *prm)
    rl, vl, kkl, gl, dec_l, key_l, icl_l = _rwkv_inputs(zl, *prm)
    r_k = r_k.astype(jnp.float32)
    state0 = jnp.zeros((zl.shape[0], RW_HEADS, RW_HEAD, RW_HEAD), jnp.float32)
    y_c = y_l = b_c = b_l = 0.0
    for d in range(2):
        rev = d == 1
        state_ctx, yc_d = _rwkv_scan(state0, rc, dec_c[d], key_c[d], vc, kkc, icl_c[d], rev)
        _, yl_d = _rwkv_scan(state_ctx, rl, dec_l[d], key_l[d], vl, kkl, icl_l[d], rev)
        y_c = y_c + yc_d
        y_l = y_l + yl_d
        b_c = b_c + jnp.sum(rc * key_c[d] * r_k, axis=-1, keepdims=True) * vc
        b_l = b_l + jnp.sum(rl * key_l[d] * r_k, axis=-1, keepdims=True) * vl
    out_l = _rwkv_readout(y_l, b_l, gl, ln_g, ln_b)
    out_c = _rwkv_readout(y_c, b_c, gc, ln_g, ln_b) if ctx_out else None
    return out_c, out_l


def conv_branch(z, dw_w, dw_b, ln_g, ln_b):
    a, b = jnp.split(z, 2, axis=-1)
    u = depthwise_conv(a * jax.nn.sigmoid(b), dw_w, dw_b)
    return jax.nn.silu(layer_norm(u, ln_g, ln_b))


def attention_branch(zc, zl, q_g, k_g, cos, sin, ctx_out):
    def split_heads(z):
        B, L, _ = z.shape
        q, k, v = jnp.split(z, (Q_COLS, Q_COLS + KV_COLS), axis=-1)
        q = rms_norm(q.reshape(B, L, ATT_KV_HEADS, ATT_GROUP, ATT_HEAD), q_g)
        k = rms_norm(k.reshape(B, L, ATT_KV_HEADS, ATT_HEAD), k_g)
        return q, k, v.reshape(B, L, ATT_KV_HEADS, ATT_HEAD)

    qc, kc, vc = split_heads(zc)
    ql, kl, vl = split_heads(zl)
    ql = apply_rope(ql, cos, sin)
    kl = apply_rope(kl, cos, sin)
    k_all = jnp.concatenate([kc, kl], axis=1)
    v_all = jnp.concatenate([vc, vl], axis=1)
    B, S = zl.shape[:2]
    n_blocks = S // Q_BLOCK
    q_blocks = jnp.swapaxes(ql.reshape(B, n_blocks, Q_BLOCK, ATT_KV_HEADS, ATT_GROUP, ATT_HEAD), 0, 1)
    o_blocks = lax.map(lambda qb: sdpa(qb, k_all, v_all), q_blocks)
    out_l = jnp.swapaxes(o_blocks, 0, 1).reshape(B, S, Q_COLS)
    if not ctx_out:
        return None, out_l
    out_c = sdpa(qc, kc, vc).reshape(B, zc.shape[1], Q_COLS)
    return out_c, out_l


def merge_branches(o_rw, o_cv, o_at, z_gate, w_branch, w_out):
    branches = jnp.stack([o_rw, o_cv.astype(o_rw.dtype), o_at.astype(o_rw.dtype)], axis=-2)
    proj = jnp.einsum('blnc,ncd->blnd', branches, w_branch)
    gates = jax.nn.sigmoid(z_gate.reshape(proj.shape).astype(jnp.float32)).astype(proj.dtype)
    return jnp.sum(gates * proj, axis=-2) @ w_out


def token_mixer(hc, hl, cos, sin, w_in, rw, cv, at, w_branch, w_out, ctx_out):
    zc = hc @ w_in
    zl = hl @ w_in
    rwc, cvc, atc, gtc = jnp.split(zc, IN_SPLIT_POINTS, axis=-1)
    rwl, cvl, atl, gtl = jnp.split(zl, IN_SPLIT_POINTS, axis=-1)
    o_rw_c, o_rw_l = rwkv_branch(rwc, rwl, *rw, ctx_out)
    o_at_c, o_at_l = attention_branch(atc, atl, *at, cos, sin, ctx_out)
    o_cv_l = conv_branch(cvl, *cv)
    out_l = merge_branches(o_rw_l, o_cv_l, o_at_l, gtl, w_branch, w_out)
    if not ctx_out:
        return None, out_l
    o_cv_c = conv_branch(cvc, *cv)
    out_c = merge_branches(o_rw_c, o_cv_c, o_at_c, gtc, w_branch, w_out)
    return out_c, out_l


def sq_relu_mlp(h, w1, w2):
    return jnp.square(jax.nn.relu(h @ w1)) @ w2


def setup_inputs(seed: int = 0) -> dict:
    key = jax.random.key(seed)
    ks = jax.random.split(key, 32)
    L, D = DEPTH, D_MODEL

    def nrm(k, shape, s):
        return jax.random.normal(k, shape, jnp.float32) * s

    return {
        'x': nrm(ks[0], (BATCH, SEQ, D), 1.0),
        'c': nrm(ks[1], (BATCH, D), 1.0),
        'ctx': nrm(ks[2], (BATCH, CTX_LEN, D), 1.0),
        'c_ctx': nrm(ks[3], (D,), 1.0),
        'w_mod': nrm(ks[4], (L, D, N_MOD * D), 0.5 * D ** -0.5),
        'b_mod': nrm(ks[5], (L, N_MOD * D), 0.01),
        'norm_mix_pre': 1.0 + nrm(ks[6], (L, D), 0.05),
        'norm_mix_post': 1.0 + nrm(ks[7], (L, D), 0.05),
        'norm_mlp_pre': 1.0 + nrm(ks[8], (L, D), 0.05),
        'norm_mlp_post': 1.0 + nrm(ks[9], (L, D), 0.05),
        'w_in': nrm(ks[10], (L, D, IN_COLS), D ** -0.5),
        'rw_mu': jax.random.uniform(ks[11], (L, RWKV_COLS), jnp.float32),
        'rw_w0': nrm(ks[12], (L, 2, RW_DIM), 1.0) - 1.0,
        'rw_w2': nrm(ks[13], (L, 2, DECAY_LORA, RW_DIM), DECAY_LORA ** -0.5),
        'rw_a0': nrm(ks[14], (L, 2, RW_DIM), 0.1),
        'rw_a2': nrm(ks[15], (L, 2, ICLR_LORA, RW_DIM), ICLR_LORA ** -0.5),
        'rw_g2': nrm(ks[16], (L, GATE_LORA, RW_DIM), GATE_LORA ** -0.5),
        'rw_k_k': 0.85 + nrm(ks[17], (L, RW_DIM), 0.05),
        'rw_k_a': 1.0 + nrm(ks[18], (L, RW_DIM), 0.05),
        'rw_r_k': nrm(ks[19], (L, RW_HEADS, RW_HEAD), 0.1),
        'rw_ln_g': 1.0 + nrm(ks[20], (L, RW_DIM), 0.05),
        'rw_ln_b': nrm(ks[21], (L, RW_DIM), 0.01),
        'cv_dw_w': nrm(ks[22], (L, CONV_WIDTH, CONV_DIM), CONV_WIDTH ** -0.5),
        'cv_dw_b': nrm(ks[23], (L, CONV_DIM), 0.01),
        'cv_ln_g': 1.0 + nrm(ks[24], (L, CONV_DIM), 0.05),
        'cv_ln_b': nrm(ks[25], (L, CONV_DIM), 0.01),
        'at_q_norm': 1.0 + nrm(ks[26], (L, ATT_HEAD), 0.05),
        'at_k_norm': 1.0 + nrm(ks[27], (L, ATT_HEAD), 0.05),
        'w_branch': nrm(ks[28], (L, N_BRANCH, D, D), D ** -0.5),
        'w_out': nrm(ks[29], (L, D, D), D ** -0.5),
        'w_ff1': nrm(ks[30], (L, D, D_FF), D ** -0.5),
        'w_ff2': nrm(ks[31], (L, D_FF, D), D_FF ** -0.5),
    }


def reference(x, c, ctx, c_ctx, w_mod, b_mod, norm_mix_pre, norm_mix_post, norm_mlp_pre, norm_mlp_post,
              w_in, rw_mu, rw_w0, rw_w2, rw_a0, rw_a2, rw_g2, rw_k_k, rw_k_a, rw_r_k, rw_ln_g, rw_ln_b,
              cv_dw_w, cv_dw_b, cv_ln_g, cv_ln_b, at_q_norm, at_k_norm, w_branch, w_out, w_ff1, w_ff2):
    rows = x.shape[1] // GRID_W
    cos, sin = axial_rope_tables(rows)
    silu_c = jax.nn.silu(c)[:, None, :]
    silu_cc = jax.nn.silu(c_ctx)
    xc, xl = ctx, x
    for l in range(DEPTH):
        ctx_out = l < DEPTH - 1
        mod_l = jnp.split(silu_c @ w_mod[l] + b_mod[l], N_MOD, axis=-1)
        mod_c = jnp.split(silu_cc @ w_mod[l] + b_mod[l], N_MOD, axis=-1)
        rw = (rw_mu[l], rw_w0[l], rw_w2[l], rw_a0[l], rw_a2[l], rw_g2[l], rw_k_k[l], rw_k_a[l],
              rw_r_k[l], rw_ln_g[l], rw_ln_b[l])
        cv = (cv_dw_w[l], cv_dw_b[l], cv_ln_g[l], cv_ln_b[l])
        at = (at_q_norm[l], at_k_norm[l])
        hc = modulate(xc, norm_mix_pre[l], mod_c[0], mod_c[1])
        hl = modulate(xl, norm_mix_pre[l], mod_l[0], mod_l[1])
        oc, ol = token_mixer(hc, hl, cos, sin, w_in[l], rw, cv, at, w_branch[l], w_out[l], ctx_out)
        xl = xl + mod_l[2] * rms_norm(ol, norm_mix_post[l])
        hl = modulate(xl, norm_mlp_pre[l], mod_l[3], mod_l[4])
        xl = xl + mod_l[5] * rms_norm(sq_relu_mlp(hl, w_ff1[l], w_ff2[l]), norm_mlp_post[l])
        if ctx_out:
            xc = xc + mod_c[2] * rms_norm(oc, norm_mix_post[l])
            hc = modulate(xc, norm_mlp_pre[l], mod_c[3], mod_c[4])
            xc = xc + mod_c[5] * rms_norm(sq_relu_mlp(hc, w_ff1[l], w_ff2[l]), norm_mlp_post[l])
    return xl
```

```python
import functools
import math

import jax
import jax.numpy as jnp
from jax import lax
from jax.experimental import pallas as pl
from jax.experimental.pallas import tpu as pltpu

F32 = jnp.float32
BF = jnp.bfloat16

D_MODEL = 1024
N_MOD = 6
EPS = 1e-6
RW_HEAD = 64
RW_HEADS = 16
RW_GN_EPS = 64e-5
RW_LORA_COLS = 384
RWKV_COLS = 3 * D_MODEL + RW_LORA_COLS
CONV_WIDTH = 31
CONV_HALO = 16
ATT_HEAD = 128
ATT_Q_HEADS = 8
ATT_KV_HEADS = 2
ATT_GROUP = ATT_Q_HEADS // ATT_KV_HEADS
KV_COLS = ATT_KV_HEADS * ATT_HEAD
ATT_COLS = D_MODEL + 2 * KV_COLS
GRID_W = 64
ROPE_THETA = 10000.0
D_FF = 4 * D_MODEL

LANES = 128
SUBLANES = 8
CHUNK = 64
TOKEN_BLOCK = 256
PROJ_ROWS = 1152
VMEM_LIMIT = 56 * 1024 * 1024
EXP_NEG_HALF = math.exp(-0.5)


def _cparams(sem):
    return pltpu.CompilerParams(dimension_semantics=sem, vmem_limit_bytes=VMEM_LIMIT)


def _dot(a, b):
    return jnp.dot(a.astype(BF), b.astype(BF), preferred_element_type=F32)


def _dot_nt(a, b):
    return lax.dot_general(a.astype(BF), b.astype(BF), (((1,), (1,)), ((), ())), preferred_element_type=F32)


def _dot_tn(a, b):
    return lax.dot_general(a.astype(BF), b.astype(BF), (((0,), (0,)), ((), ())), preferred_element_type=F32)


def _split_dot(x, w):
    hi = x.astype(BF)
    lo = (x - hi.astype(F32)).astype(BF)
    return jnp.dot(hi, w, preferred_element_type=F32) + jnp.dot(lo, w, preferred_element_type=F32)


def _sigmoid(x):
    return 1.0 / (1.0 + jnp.exp(-x))


def _rms(x, g):
    return x * lax.rsqrt(jnp.mean(x * x, axis=-1, keepdims=True) + EPS) * g


def _mod_kernel(c_ref, w_ref, b_ref, o_ref):
    c = c_ref[...]
    o_ref[0] = _dot(c * _sigmoid(c), w_ref[0]) + b_ref[0]


def _modulation(cvec, w_mod, b_mod):
    depth, d, n = w_mod.shape
    rows = cvec.shape[0]
    tn = 1536
    return pl.pallas_call(
        _mod_kernel,
        out_shape=jax.ShapeDtypeStruct((depth, rows, n), F32),
        grid=(depth, n // tn),
        in_specs=[
            pl.BlockSpec((rows, d), lambda l, j: (0, 0)),
            pl.BlockSpec((1, d, tn), lambda l, j: (l, 0, j)),
            pl.BlockSpec((1, 1, tn), lambda l, j: (l, 0, j)),
        ],
        out_specs=pl.BlockSpec((1, rows, tn), lambda l, j: (l, 0, j)),
        compiler_params=_cparams(("parallel", "parallel")),
        name="modulation",
    )(cvec, w_mod, b_mod.reshape(depth, 1, n))


def _mod_rows(ml_ref, mc_ref, idx, is_ctx):
    return jnp.where(is_ctx, mc_ref[idx:idx + 1, :], ml_ref[0, idx:idx + 1, :])


def _modproj_kernel(x_ref, g_ref, ml_ref, mc_ref, w_ref, o_ref, h_ref, *, shift_idx, scale_idx, blocks_per_seq, ctx_len):
    i = pl.program_id(0)

    @pl.when(pl.program_id(1) == 0)
    def _():
        tm = x_ref.shape[0]
        xn = _rms(x_ref[...], g_ref[...])
        row = (i % blocks_per_seq) * tm + lax.broadcasted_iota(jnp.int32, (tm, 1), 0)
        is_ctx = row < ctx_len
        scale = _mod_rows(ml_ref, mc_ref, scale_idx, is_ctx)
        shift = _mod_rows(ml_ref, mc_ref, shift_idx, is_ctx)
        h_ref[...] = (xn * (1.0 + scale) + shift).astype(BF)

    o_ref[...] = jnp.dot(h_ref[...], w_ref[...], preferred_element_type=F32)


def _modproj(x2, gain, modl, modc, w, *, tn, seq_len, ctx_len, shift_idx, scale_idx):
    m, d = x2.shape
    n = w.shape[1]
    tm = max(t for t in range(LANES, PROJ_ROWS + 1, LANES) if seq_len % t == 0)
    bps = seq_len // tm
    kern = functools.partial(_modproj_kernel, shift_idx=shift_idx, scale_idx=scale_idx,
                             blocks_per_seq=bps, ctx_len=ctx_len)
    return pl.pallas_call(
        kern,
        out_shape=jax.ShapeDtypeStruct((m, n), F32),
        grid=(m // tm, n // tn),
        in_specs=[
            pl.BlockSpec((tm, d), lambda i, j: (i, 0)),
            pl.BlockSpec((1, d), lambda i, j: (0, 0)),
            pl.BlockSpec((1, SUBLANES, d), lambda i, j: (i // bps, 0, 0)),
            pl.BlockSpec((SUBLANES, d), lambda i, j: (0, 0)),
            pl.BlockSpec((d, tn), lambda i, j: (0, j)),
        ],
        out_specs=pl.BlockSpec((tm, tn), lambda i, j: (i, j)),
        scratch_shapes=[pltpu.VMEM((tm, d), BF)],
        compiler_params=_cparams(("parallel", "arbitrary")),
        name="modproj",
    )(x2, gain, modl, modc, w)


def _rwkv_prep_kernel(z_ref, zp_ref, zn_ref, mu_ref, vec_ref, w2_ref, a2_ref, g2_ref, ones_ref, trif_ref, trib_ref,
                      v_ref, pf_ref, qf_ref, kf_ref, rf_ref, pb_ref, qb_ref, kb_ref, rb_ref, wc_ref, g_ref, bon_ref,
                      *, nblocks):
    i = pl.program_id(1)
    z = z_ref[0]
    tb = z.shape[0]
    row = lax.broadcasted_iota(jnp.int32, (tb, 1), 0)
    prev_ok = i >= 2
    next_ok = jnp.logical_and(i >= 1, i <= nblocks - 2)
    zp_row = jnp.where(prev_ok, zp_ref[0, SUBLANES - 1:SUBLANES, :], 0.0)
    zn_row = jnp.where(next_ok, zn_ref[0, 0:1, :], 0.0)
    zprev = jnp.where(row == 0, zp_row, pltpu.roll(z, 1, 0))
    znext = jnp.where(row == tb - 1, zn_row, pltpu.roll(z, tb - 1, 0))
    zs = z + mu_ref[...] * (0.5 * (zprev + znext) - z)

    d = D_MODEL
    r = zs[:, 0:d]
    k = zs[:, d:2 * d]
    v = zs[:, 2 * d:3 * d]
    w_lo = _dot(jnp.tanh(zs[:, 3 * d:3 * d + 128]), w2_ref[...])
    a_lo = _dot(zs[:, 3 * d + 128:3 * d + 256], a2_ref[...])
    g_ref[0] = _dot(_sigmoid(zs[:, 3 * d + 256:3 * d + 384]), g2_ref[...])

    vec = vec_ref[...]
    ones = ones_ref[...]
    kk0 = k * vec[4:5]
    kk = kk0 * jnp.minimum(lax.rsqrt(_split_dot(kk0 * kk0, ones)), 1e12)

    keysum = None
    outs = ((pf_ref, qf_ref, kf_ref, rf_ref, trif_ref), (pb_ref, qb_ref, kb_ref, rb_ref, trib_ref))
    for dr, (p_ref, q_ref, k_ref, r_ref, tri_ref) in enumerate(outs):
        w_raw = vec[dr:dr + 1] + w_lo[:, dr * d:(dr + 1) * d]
        lw = -EXP_NEG_HALF * _sigmoid(w_raw)
        a = _sigmoid(vec[2 + dr:3 + dr] + a_lo[:, dr * d:(dr + 1) * d])
        key = k * (1.0 + (a - 1.0) * vec[5:6])
        cum = _split_dot_left(tri_ref[...], lw)
        wt = jnp.exp(cum)
        iw = jnp.exp(-cum)
        p_ref[0] = (kk * jnp.exp(cum - lw)).astype(BF)
        q_ref[0] = (a * kk * iw).astype(BF)
        k_ref[0] = (key * iw).astype(BF)
        r_ref[0] = (r * wt).astype(BF)
        for c in range(tb // CHUNK):
            last = c * CHUNK + (CHUNK - 1 if dr == 0 else 0)
            wc_ref[0, 0, dr * 4 + c:dr * 4 + c + 1, :] = wt[last:last + 1, :]
        keysum = key if keysum is None else keysum + key

    bon_ref[0] = _dot(r * keysum * vec[6:7], ones) * v
    v_ref[0] = v.astype(BF)


def _split_dot_left(w, x):
    hi = x.astype(BF)
    lo = (x - hi.astype(F32)).astype(BF)
    return jnp.dot(w, hi, preferred_element_type=F32) + jnp.dot(w, lo, preferred_element_type=F32)


def _rwkv_prep(z_rw, mu, vec, w2cat, a2cat, g2, ones, trif, trib):
    b, l, cols = z_rw.shape
    tb = TOKEN_BLOCK
    nb = l // tb
    d = D_MODEL
    hb = tb // SUBLANES
    nhalo = l // SUBLANES
    tok = lambda bi, i: (bi, i, 0)
    const2 = lambda bi, i: (0, 0)
    seq_bf = jax.ShapeDtypeStruct((b, l, d), BF)
    seq_f32 = jax.ShapeDtypeStruct((b, l, d), F32)
    out_tok = pl.BlockSpec((1, tb, d), tok)
    kern = functools.partial(_rwkv_prep_kernel, nblocks=nb)
    return pl.pallas_call(
        kern,
        out_shape=[seq_bf] * 9 + [jax.ShapeDtypeStruct((b, nb, SUBLANES, d), F32), seq_f32, seq_f32],
        grid=(b, nb),
        in_specs=[
            pl.BlockSpec((1, tb, cols), tok),
            pl.BlockSpec((1, SUBLANES, cols), lambda bi, i: (bi, jnp.maximum(i * hb - 1, 0), 0)),
            pl.BlockSpec((1, SUBLANES, cols), lambda bi, i: (bi, jnp.minimum((i + 1) * hb, nhalo - 1), 0)),
            pl.BlockSpec((1, cols), const2),
            pl.BlockSpec((SUBLANES, d), const2),
            pl.BlockSpec((LANES, 2 * d), const2),
            pl.BlockSpec((LANES, 2 * d), const2),
            pl.BlockSpec((LANES, d), const2),
            pl.BlockSpec((d, d), const2),
            pl.BlockSpec((tb, tb), const2),
            pl.BlockSpec((tb, tb), const2),
        ],
        out_specs=[out_tok] * 9 + [pl.BlockSpec((1, 1, SUBLANES, d), lambda bi, i: (bi, i, 0, 0)), out_tok, out_tok],
        compiler_params=_cparams(("parallel", "parallel")),
        name="rwkv_prep",
    )(z_rw, z_rw, z_rw, mu, vec, w2cat, a2cat, g2, ones, trif, trib)


def _pair_chunk(p2, q2, k2, r2, v2, s2, wrow, strict, incl, blk, eye, m_a, bd):
    c = CHUNK

    def sm(x):
        return jnp.concatenate([jnp.where(m_a, x, 0), jnp.where(m_a, 0, x)], axis=0).astype(BF)

    def pm(x, y):
        return jnp.dot(x.astype(BF), sm(y), preferred_element_type=F32)

    pr = jnp.concatenate([p2, r2], axis=0)
    gram = _dot_nt(pr, jnp.concatenate([sm(q2), sm(k2)], axis=0))
    a_pq, a_pk = gram[:c, :2 * c], gram[:c, 2 * c:]
    a_rq, a_rk = gram[c:, :2 * c], gram[c:, 2 * c:]
    prs = _dot_nt(pr, s2)
    lm = jnp.where(strict, a_pq, 0.0)
    rhs = prs[:c] + pm(jnp.where(strict, a_pk, 0.0), v2)
    dm = jnp.where(blk, lm, 0.0)
    em = lm - dm
    n1 = eye - dm
    d2 = pm(dm, dm)
    n2 = n1 + pm(n1, d2)
    d4 = pm(d2, d2)
    n3 = n2 + pm(n2, d4)
    td = n3 + pm(n3, pm(d4, d4))
    f = pm(td, em)
    x1 = pm(td, rhs)
    zz = x1 + pm(pm(f, f), x1)
    u = pm(f, zz) - zz
    a2 = jnp.concatenate([jnp.where(incl, a_rq, 0.0), jnp.where(incl, a_rk, 0.0)], axis=1)
    y = prs[c:] + _dot(a2, jnp.concatenate([sm(u), sm(v2)], axis=0))
    upd = _dot_tn(jnp.concatenate([u.astype(BF), v2], axis=0), jnp.concatenate([q2, k2], axis=0))
    s_new = jnp.where(bd, s2 + upd, 0.0) * wrow
    return y, s_new


def _rwkv_scan_kernel(pf_ref, qf_ref, kf_ref, rf_ref, vf_ref, wcf_ref, pb_ref, qb_ref, kb_ref, rb_ref, vb_ref, wcb_ref,
                      yf_ref, yb_ref, sf_ref, sb_ref, *, bwd_chunk):
    n = pl.program_id(1)
    c = CHUNK
    npairs = sf_ref.shape[0]

    @pl.when(n == 0)
    def _():
        sf_ref[...] = jnp.zeros_like(sf_ref)
        sb_ref[...] = jnp.zeros_like(sb_ref)

    t_i = lax.broadcasted_iota(jnp.int32, (c, 2 * c), 0)
    s_i = jnp.bitwise_and(lax.broadcasted_iota(jnp.int32, (c, 2 * c), 1), c - 1)
    m_a = lax.broadcasted_iota(jnp.int32, (1, LANES), 1) < RW_HEAD
    blk = (t_i // 16) == (s_i // 16)
    eye = (t_i == s_i).astype(F32)
    bd = (lax.broadcasted_iota(jnp.int32, (LANES, LANES), 0) // RW_HEAD) == \
         (lax.broadcasted_iota(jnp.int32, (LANES, LANES), 1) // RW_HEAD)
    per_blk = TOKEN_BLOCK // c
    wrow_f = wcf_ref[0, 0, pl.ds(n % per_blk, 1), :]
    wrow_b = wcb_ref[0, 0, pl.ds(per_blk + bwd_chunk(n) % per_blk, 1), :]

    dirs = (
        (pf_ref, qf_ref, kf_ref, rf_ref, vf_ref, wrow_f, yf_ref, sf_ref, s_i < t_i, s_i <= t_i),
        (pb_ref, qb_ref, kb_ref, rb_ref, vb_ref, wrow_b, yb_ref, sb_ref, s_i > t_i, s_i >= t_i),
    )
    for p_ref, q_ref, k_ref, r_ref, v_ref, wrow, y_ref, s_ref, strict, incl in dirs:
        for p in range(npairs):
            sl = slice(p * LANES, (p + 1) * LANES)
            y, s_new = _pair_chunk(p_ref[0, :, sl], q_ref[0, :, sl], k_ref[0, :, sl], r_ref[0, :, sl], v_ref[0, :, sl],
                                   s_ref[p], wrow[:, sl], strict, incl, blk, eye, m_a, bd)
            y_ref[0, :, sl] = y
            s_ref[p] = s_new


def _rwkv_scan(v, pf, qf, kf, rf, pb, qb, kb, rb, wc, *, ctx_len):
    b, l, d = v.shape
    c = CHUNK
    nch = l // c
    nctx = ctx_len // c
    per_blk = TOKEN_BLOCK // c

    def bwd_chunk(n):
        return jnp.where(n < nctx, nctx - 1 - n, nch + nctx - 1 - n)

    fwd = pl.BlockSpec((1, c, d), lambda bi, n: (bi, n, 0))
    bwd = pl.BlockSpec((1, c, d), lambda bi, n: (bi, bwd_chunk(n), 0))
    wcf = pl.BlockSpec((1, 1, SUBLANES, d), lambda bi, n: (bi, n // per_blk, 0, 0))
    wcb = pl.BlockSpec((1, 1, SUBLANES, d), lambda bi, n: (bi, bwd_chunk(n) // per_blk, 0, 0))
    npairs = d // LANES
    kern = functools.partial(_rwkv_scan_kernel, bwd_chunk=bwd_chunk)
    return pl.pallas_call(
        kern,
        out_shape=[jax.ShapeDtypeStruct((b, l, d), F32)] * 2,
        grid=(b, nch),
        in_specs=[fwd] * 5 + [wcf] + [bwd] * 5 + [wcb],
        out_specs=[fwd, bwd],
        scratch_shapes=[pltpu.VMEM((npairs, LANES, LANES), F32)] * 2,
        compiler_params=_cparams(("parallel", "arbitrary")),
        name="rwkv_scan",
    )(pf, qf, kf, rf, v, wc, pb, qb, kb, rb, v, wc)


def _rwkv_readout_kernel(yf_ref, yb_ref, bon_ref, g_ref, vec_ref, ones_ref, o_ref):
    ones = ones_ref[...]
    y = yf_ref[...] + yb_ref[...]
    inv_n = 1.0 / RW_HEAD
    yc = y - _split_dot(y, ones) * inv_n
    var = _split_dot(yc * yc, ones) * inv_n
    yn = yc * lax.rsqrt(var + RW_GN_EPS) * vec_ref[0:1, :] + vec_ref[1:2, :]
    o_ref[...] = ((yn + bon_ref[...]) * g_ref[...]).astype(BF)


def _rwkv_readout(yf, yb, bon, g, vec, ones):
    m, d = yf.shape
    tm = TOKEN_BLOCK
    tok = pl.BlockSpec((tm, d), lambda i: (i, 0))
    return pl.pallas_call(
        _rwkv_readout_kernel,
        out_shape=jax.ShapeDtypeStruct((m, d), BF),
        grid=(m // tm,),
        in_specs=[tok, tok, tok, tok, pl.BlockSpec((SUBLANES, d), lambda i: (0, 0)), pl.BlockSpec((d, d), lambda i: (0, 0))],
        out_specs=tok,
        compiler_params=_cparams(("parallel",)),
        name="rwkv_readout",
    )(yf, yb, bon, g, vec, ones)


CONV_ROWS = 32
CONV_LANES = 512


def _conv_kernel(z_ref, zp_ref, zn_ref, w_ref, vec_ref, o_ref, gp_ref, u_ref, *, nblocks):
    i = pl.program_id(1)
    tb = z_ref.shape[1]
    d = D_MODEL
    h = CONV_HALO

    def glu(zz):
        return zz[:, :d] * _sigmoid(zz[:, d:])

    prev_ok = i >= 2
    next_ok = jnp.logical_and(i >= 1, i <= nblocks - 2)
    gp_ref[0:h, :] = jnp.where(prev_ok, glu(zp_ref[0]), 0.0)
    gp_ref[h:h + tb, :] = glu(z_ref[0])
    gp_ref[h + tb:2 * h + tb, :] = jnp.where(next_ok, glu(zn_ref[0]), 0.0)

    off = h - CONV_WIDTH // 2
    for r0 in range(0, tb, CONV_ROWS):
        for c0 in range(0, d, CONV_LANES):
            acc = jnp.zeros((CONV_ROWS, CONV_LANES), F32)
            for k in range(CONV_WIDTH):
                acc = acc + gp_ref[r0 + k + off:r0 + k + off + CONV_ROWS, c0:c0 + CONV_LANES] * w_ref[k:k + 1, c0:c0 + CONV_LANES]
            u_ref[r0:r0 + CONV_ROWS, c0:c0 + CONV_LANES] = acc

    u = u_ref[...] + vec_ref[0:1, :]
    mu = jnp.mean(u, axis=-1, keepdims=True)
    uc = u - mu
    var = jnp.mean(uc * uc, axis=-1, keepdims=True)
    y = uc * lax.rsqrt(var + EPS) * vec_ref[1:2, :] + vec_ref[2:3, :]
    o_ref[0] = (y * _sigmoid(y)).astype(BF)


def _conv_branch(z_cv, w, vec):
    b, l, cols = z_cv.shape
    tb = TOKEN_BLOCK
    nb = l // tb
    d = D_MODEL
    hb = tb // CONV_HALO
    nhalo = l // CONV_HALO
    kern = functools.partial(_conv_kernel, nblocks=nb)
    return pl.pallas_call(
        kern,
        out_shape=jax.ShapeDtypeStruct((b, l, d), BF),
        grid=(b, nb),
        in_specs=[
            pl.BlockSpec((1, tb, cols), lambda bi, i: (bi, i, 0)),
            pl.BlockSpec((1, CONV_HALO, cols), lambda bi, i: (bi, jnp.maximum(i * hb - 1, 0), 0)),
            pl.BlockSpec((1, CONV_HALO, cols), lambda bi, i: (bi, jnp.minimum((i + 1) * hb, nhalo - 1), 0)),
            pl.BlockSpec((4 * SUBLANES, d), lambda bi, i: (0, 0)),
            pl.BlockSpec((SUBLANES, d), lambda bi, i: (0, 0)),
        ],
        out_specs=pl.BlockSpec((1, tb, d), lambda bi, i: (bi, i, 0)),
        scratch_shapes=[pltpu.VMEM((tb + 2 * CONV_HALO, d), F32), pltpu.VMEM((tb, d), F32)],
        compiler_params=_cparams(("parallel", "parallel")),
        name="conv_branch",
    )(z_cv, z_cv, z_cv, w, vec)


def _att_prep_kernel(z_ref, cos_ref, sin_ref, gq_ref, gk_ref, q_ref, k_ref, v_ref):
    cos = cos_ref[...]
    sin = sin_ref[...]
    hd = ATT_HEAD

    def norm_rope(x, g):
        xn = _rms(x, g)
        return xn * cos + pltpu.roll(xn, hd // 2, 1) * sin

    for h in range(ATT_Q_HEADS):
        sl = slice(h * hd, (h + 1) * hd)
        q_ref[0, :, sl] = (norm_rope(z_ref[0, :, sl], gq_ref[...]) * (hd ** -0.5)).astype(BF)
    for h in range(ATT_KV_HEADS):
        sl = slice(h * hd, (h + 1) * hd)
        k_ref[0, :, sl] = norm_rope(z_ref[0, :, D_MODEL + h * hd:D_MODEL + (h + 1) * hd], gk_ref[...]).astype(BF)
    v_ref[0] = z_ref[0, :, D_MODEL + KV_COLS:].astype(BF)


def _att_prep(z_at, cos_t, sin_t, gq, gk):
    b, l, cols = z_at.shape
    tb = TOKEN_BLOCK
    tok = lambda bi, i: (bi, i, 0)
    return pl.pallas_call(
        _att_prep_kernel,
        out_shape=[jax.ShapeDtypeStruct((b, l, D_MODEL), BF), jax.ShapeDtypeStruct((b, l, KV_COLS), BF),
                   jax.ShapeDtypeStruct((b, l, KV_COLS), BF)],
        grid=(b, l // tb),
        in_specs=[
            pl.BlockSpec((1, tb, cols), tok),
            pl.BlockSpec((tb, ATT_HEAD), lambda bi, i: (i, 0)),
            pl.BlockSpec((tb, ATT_HEAD), lambda bi, i: (i, 0)),
            pl.BlockSpec((1, ATT_HEAD), lambda bi, i: (0, 0)),
            pl.BlockSpec((1, ATT_HEAD), lambda bi, i: (0, 0)),
        ],
        out_specs=[pl.BlockSpec((1, tb, D_MODEL), tok), pl.BlockSpec((1, tb, KV_COLS), tok), pl.BlockSpec((1, tb, KV_COLS), tok)],
        compiler_params=_cparams(("parallel", "parallel")),
        name="att_prep",
    )(z_at, cos_t, sin_t, gq, gk)


def _attn_kernel(q_ref, k_ref, v_ref, o_ref, *, ctx_len):
    hd = ATT_HEAD

    def run(nk):
        for hk in range(ATT_KV_HEADS):
            kh = k_ref[0, 0:nk, hk * hd:(hk + 1) * hd]
            vh = v_ref[0, 0:nk, hk * hd:(hk + 1) * hd]
            for g in range(ATT_GROUP):
                sl = slice((hk * ATT_GROUP + g) * hd, (hk * ATT_GROUP + g + 1) * hd)
                s = _dot_nt(q_ref[0, :, sl], kh)
                p = jnp.exp(s - jnp.max(s, axis=-1, keepdims=True))
                o = jnp.dot(p.astype(BF), vh, preferred_element_type=F32) / jnp.sum(p, axis=-1, keepdims=True)
                o_ref[0, :, sl] = o.astype(BF)

    @pl.when(pl.program_id(1) == 0)
    def _():
        run(ctx_len)

    @pl.when(pl.program_id(1) > 0)
    def _():
        run(k_ref.shape[1])


def _attention(q, k, v, *, ctx_len):
    b, l, d = q.shape
    tb = TOKEN_BLOCK
    kern = functools.partial(_attn_kernel, ctx_len=ctx_len)
    kv = pl.BlockSpec((1, l, KV_COLS), lambda bi, i: (bi, 0, 0))
    return pl.pallas_call(
        kern,
        out_shape=jax.ShapeDtypeStruct((b, l, d), BF),
        grid=(b, l // tb),
        in_specs=[pl.BlockSpec((1, tb, d), lambda bi, i: (bi, i, 0)), kv, kv],
        out_specs=pl.BlockSpec((1, tb, d), lambda bi, i: (bi, i, 0)),
        compiler_params=_cparams(("parallel", "parallel")),
        name="attention",
    )(q, k, v)


def _merge_kernel(orw_ref, ocv_ref, oat_ref, zg_ref, x_ref, wb_ref, wo_ref, g_ref, ml_ref, mc_ref, o_ref, *, blocks_per_seq):
    d = D_MODEL
    is_ctx = (pl.program_id(0) % blocks_per_seq) == 0
    m = None
    for n, o_n in enumerate((orw_ref, ocv_ref, oat_ref)):
        t = _sigmoid(zg_ref[:, n * d:(n + 1) * d]) * jnp.dot(o_n[...], wb_ref[n], preferred_element_type=F32)
        m = t if m is None else m + t
    out = _dot(m, wo_ref[...])
    o_ref[...] = x_ref[...] + _mod_rows(ml_ref, mc_ref, 2, is_ctx) * _rms(out, g_ref[...])


def _merge(orw, ocv, oat, zg, x2, wb, wo, gain, modl, modc, *, seq_len):
    m, d = x2.shape
    tm = TOKEN_BLOCK
    bps = seq_len // tm
    tok = pl.BlockSpec((tm, d), lambda i: (i, 0))
    kern = functools.partial(_merge_kernel, blocks_per_seq=bps)
    return pl.pallas_call(
        kern,
        out_shape=jax.ShapeDtypeStruct((m, d), F32),
        grid=(m // tm,),
        in_specs=[
            tok, tok, tok,
            pl.BlockSpec((tm, 3 * d), lambda i: (i, 0)),
            tok,
            pl.BlockSpec((3, d, d), lambda i: (0, 0, 0)),
            pl.BlockSpec((d, d), lambda i: (0, 0)),
            pl.BlockSpec((1, d), lambda i: (0, 0)),
            pl.BlockSpec((1, SUBLANES, d), lambda i: (i // bps, 0, 0)),
            pl.BlockSpec((SUBLANES, d), lambda i: (0, 0)),
        ],
        out_specs=tok,
        compiler_params=_cparams(("parallel",)),
        name="merge",
    )(orw, ocv, oat, zg, x2, wb, wo, gain, modl, modc)


def _mlp_kernel(x_ref, w1_ref, w2_ref, gpre_ref, gpost_ref, ml_ref, mc_ref, o_ref, *, blocks_per_seq):
    is_ctx = (pl.program_id(0) % blocks_per_seq) == 0
    x = x_ref[...]
    h = _rms(x, gpre_ref[...]) * (1.0 + _mod_rows(ml_ref, mc_ref, 4, is_ctx)) + _mod_rows(ml_ref, mc_ref, 3, is_ctx)
    a = jnp.maximum(_dot(h, w1_ref[...]), 0.0)
    out = _dot(a * a, w2_ref[...])
    o_ref[...] = x + _mod_rows(ml_ref, mc_ref, 5, is_ctx) * _rms(out, gpost_ref[...])


def _mlp(x2, w1, w2, gpre, gpost, modl, modc, *, seq_len):
    m, d = x2.shape
    tm = TOKEN_BLOCK
    bps = seq_len // tm
    dff = w1.shape[1]
    tok = pl.BlockSpec((tm, d), lambda i: (i, 0))
    kern = functools.partial(_mlp_kernel, blocks_per_seq=bps)
    return pl.pallas_call(
        kern,
        out_shape=jax.ShapeDtypeStruct((m, d), F32),
        grid=(m // tm,),
        in_specs=[
            tok,
            pl.BlockSpec((d, dff), lambda i: (0, 0)),
            pl.BlockSpec((dff, d), lambda i: (0, 0)),
            pl.BlockSpec((1, d), lambda i: (0, 0)),
            pl.BlockSpec((1, d), lambda i: (0, 0)),
            pl.BlockSpec((1, SUBLANES, d), lambda i: (i // bps, 0, 0)),
            pl.BlockSpec((SUBLANES, d), lambda i: (0, 0)),
        ],
        out_specs=tok,
        compiler_params=_cparams(("parallel",)),
        name="mlp",
    )(x2, w1, w2, gpre, gpost, modl, modc)


def _pad_rows(a, rows):
    return jnp.pad(a, ((0, rows - a.shape[0]), (0, 0)))


def _rope_tables(seq, ctx_len):
    rows = seq // GRID_W
    row = jnp.repeat(jnp.arange(rows), GRID_W).astype(F32)
    col = jnp.tile(jnp.arange(GRID_W), rows).astype(F32)
    axis_dim = ATT_HEAD // 2
    freqs = ROPE_THETA ** (-jnp.arange(0, axis_dim, 2, dtype=F32) / axis_dim)
    ang = jnp.concatenate([row[:, None] * freqs, col[:, None] * freqs], axis=-1)
    cos, sin = jnp.cos(ang), jnp.sin(ang)
    cos_t = jnp.concatenate([cos, cos], axis=-1)
    sin_t = jnp.concatenate([-sin, sin], axis=-1)
    cos_t = jnp.concatenate([jnp.ones((ctx_len, ATT_HEAD), F32), cos_t], axis=0)
    sin_t = jnp.concatenate([jnp.zeros((ctx_len, ATT_HEAD), F32), sin_t], axis=0)
    return cos_t, sin_t


def _chunk_tri(upper):
    t = jnp.arange(TOKEN_BLOCK)
    same = (t[:, None] // CHUNK) == (t[None, :] // CHUNK)
    tri = (t[None, :] >= t[:, None]) if upper else (t[None, :] <= t[:, None])
    return jnp.logical_and(same, tri).astype(BF)


def kernel(x, c, ctx, c_ctx, w_mod, b_mod, norm_mix_pre, norm_mix_post, norm_mlp_pre, norm_mlp_post, w_in, rw_mu, rw_w0, rw_w2, rw_a0, rw_a2, rw_g2, rw_k_k, rw_k_a, rw_r_k, rw_ln_g, rw_ln_b, cv_dw_w, cv_dw_b, cv_ln_g, cv_ln_b, at_q_norm, at_k_norm, w_branch, w_out, w_ff1, w_ff2):
    b, s, d = x.shape
    ctx_len = ctx.shape[1]
    depth = w_in.shape[0]
    l = ctx_len + s
    assert d == D_MODEL and ctx_len == TOKEN_BLOCK and s % TOKEN_BLOCK == 0

    mod_rows = -(-(b + 1) // SUBLANES) * SUBLANES
    cvec = _pad_rows(jnp.concatenate([c, c_ctx[None, :]], axis=0), mod_rows)
    mods = _modulation(cvec, w_mod, b_mod).reshape(depth, mod_rows, N_MOD, d)
    mods = jnp.pad(mods, ((0, 0), (0, 0), (0, SUBLANES - N_MOD), (0, 0)))

    cos_t, sin_t = _rope_tables(s, ctx_len)
    half_perm = jnp.concatenate([jnp.arange(0, ATT_HEAD, 2), jnp.arange(1, ATT_HEAD, 2)])
    qk_perm = (jnp.arange(ATT_Q_HEADS + ATT_KV_HEADS)[:, None] * ATT_HEAD + half_perm[None, :]).reshape(-1)
    ones_blk = (jnp.arange(d)[:, None] // RW_HEAD == jnp.arange(d)[None, :] // RW_HEAD).astype(BF)
    trif, trib = _chunk_tri(False), _chunk_tri(True)
    zpad64 = jnp.zeros((64, d), F32)

    xu = jnp.concatenate([ctx, x], axis=1).reshape(b * l, d)

    for li in range(depth):
        modl, modc = mods[li, :b], mods[li, b]
        w_l = w_in[li]
        c0, c1, c2 = RWKV_COLS, RWKV_COLS + 2 * d, RWKV_COLS + 2 * d + ATT_COLS
        w_rw = w_l[:, :c0].astype(BF)
        w_cv = w_l[:, c0:c1].astype(BF)
        w_at = w_l[:, c1:c2]
        w_at = jnp.concatenate([w_at[:, qk_perm], w_at[:, d + KV_COLS:]], axis=1).astype(BF)
        w_gt = w_l[:, c2:].astype(BF)
        proj = functools.partial(_modproj, xu, norm_mix_pre[li][None, :], modl, modc,
                                 seq_len=l, ctx_len=ctx_len, shift_idx=0, scale_idx=1)
        z_rw = proj(w_rw, tn=1152).reshape(b, l, RWKV_COLS)
        z_cv = proj(w_cv, tn=1024).reshape(b, l, 2 * d)
        z_at = proj(w_at, tn=768).reshape(b, l, ATT_COLS)
        z_gt = proj(w_gt, tn=1024)

        rw_vec = _pad_rows(jnp.stack([rw_w0[li, 0], rw_w0[li, 1], rw_a0[li, 0], rw_a0[li, 1], rw_k_k[li], rw_k_a[li],
                                      rw_r_k[li].reshape(-1)]), SUBLANES)
        w2cat = jnp.concatenate([jnp.concatenate([rw_w2[li, 0], zpad64], axis=0),
                                 jnp.concatenate([zpad64, rw_w2[li, 1]], axis=0)], axis=1).astype(BF)
        a2cat = jnp.concatenate([jnp.concatenate([rw_a2[li, 0], zpad64], axis=0),
                                 jnp.concatenate([zpad64, rw_a2[li, 1]], axis=0)], axis=1).astype(BF)
        v, pf, qf, kf, rf, pb, qb, kb, rb, wc, g, bon = _rwkv_prep(
            z_rw, rw_mu[li][None, :], rw_vec, w2cat, a2cat, rw_g2[li].astype(BF), ones_blk, trif, trib)
        yf, yb = _rwkv_scan(v, pf, qf, kf, rf, pb, qb, kb, rb, wc, ctx_len=ctx_len)
        ln_vec = _pad_rows(jnp.stack([rw_ln_g[li], rw_ln_b[li]]), SUBLANES)
        o_rw = _rwkv_readout(yf.reshape(b * l, d), yb.reshape(b * l, d), bon.reshape(b * l, d), g.reshape(b * l, d),
                             ln_vec, ones_blk)

        cv_vec = _pad_rows(jnp.stack([cv_dw_b[li], cv_ln_g[li], cv_ln_b[li]]), SUBLANES)
        o_cv = _conv_branch(z_cv, _pad_rows(cv_dw_w[li], 4 * SUBLANES), cv_vec).reshape(b * l, d)

        q, k, vv = _att_prep(z_at, cos_t, sin_t, at_q_norm[li][half_perm][None, :], at_k_norm[li][half_perm][None, :])
        o_at = _attention(q, k, vv, ctx_len=ctx_len).reshape(b * l, d)

        xu = _merge(o_rw, o_cv, o_at, z_gt, xu, w_branch[li].astype(BF), w_out[li].astype(BF),
                    norm_mix_post[li][None, :], modl, modc, seq_len=l)
        xu = _mlp(xu, w_ff1[li].astype(BF), w_ff2[li].astype(BF), norm_mlp_pre[li][None, :], norm_mlp_post[li][None, :],
                  modl, modc, seq_len=l)

    return xu.reshape(b, l, d)[:, ctx_len:, :]
```

```python
import functools
import math

import jax
import jax.numpy as jnp
from jax import lax
from jax.experimental import pallas as pl
from jax.experimental.pallas import tpu as pltpu

F32 = jnp.float32
BF = jnp.bfloat16

D_MODEL = 1024
N_MOD = 6
EPS = 1e-6
RW_HEAD = 64
RW_HEADS = 16
RW_GN_EPS = 64e-5
RW_LORA_COLS = 384
RWKV_COLS = 3 * D_MODEL + RW_LORA_COLS
CONV_WIDTH = 31
HALO = 16
ONES_TILE = 256
ATT_HEAD = 128
ATT_Q_HEADS = 8
ATT_KV_HEADS = 2
ATT_GROUP = ATT_Q_HEADS // ATT_KV_HEADS
ATT_STACK = 2
KV_COLS = ATT_KV_HEADS * ATT_HEAD
ATT_COLS = D_MODEL + 2 * KV_COLS
GRID_W = 64
ROPE_THETA = 10000.0
D_FF = 4 * D_MODEL

LANES = 128
SUBLANES = 8
CHUNK = 64
TOKEN_BLOCK = 256
PROJ_ROWS = 768
VMEM_LIMIT = 56 * 1024 * 1024
EXP_NEG_HALF = math.exp(-0.5)
LOG2_E = math.log2(math.e)


def _cparams(sem):
    return pltpu.CompilerParams(dimension_semantics=sem, vmem_limit_bytes=VMEM_LIMIT)


def _dot(a, b):
    return jnp.dot(a.astype(BF), b.astype(BF), preferred_element_type=F32)


def _dot_nt(a, b):
    return lax.dot_general(a.astype(BF), b.astype(BF), (((1,), (1,)), ((), ())), preferred_element_type=F32)


def _dot_tn(a, b):
    return lax.dot_general(a.astype(BF), b.astype(BF), (((0,), (0,)), ((), ())), preferred_element_type=F32)


def _split_dot(x, w):
    hi = x.astype(BF)
    lo = (x - hi.astype(F32)).astype(BF)
    return jnp.dot(hi, w, preferred_element_type=F32) + jnp.dot(lo, w, preferred_element_type=F32)


def _head_sums(x, ones, split):
    w = ones.shape[0]
    dot = _split_dot if split else _dot
    return jnp.concatenate([dot(x[:, n * w:(n + 1) * w], ones) for n in range(x.shape[1] // w)], axis=1)


def _sigmoid(x):
    return 1.0 / (1.0 + jnp.exp(-x))


def _rms(x, g):
    return x * lax.rsqrt(jnp.mean(x * x, axis=-1, keepdims=True) + EPS) * g


def _mod_kernel(c_ref, w_ref, b_ref, o_ref):
    c = c_ref[...]
    o_ref[0] = _dot(c * _sigmoid(c), w_ref[0]) + b_ref[0]


def _modulation(cvec, w_mod, b_mod):
    depth, d, n = w_mod.shape
    rows = cvec.shape[0]
    tn = 1536
    return pl.pallas_call(
        _mod_kernel,
        out_shape=jax.ShapeDtypeStruct((depth, rows, n), F32),
        grid=(depth, n // tn),
        in_specs=[
            pl.BlockSpec((rows, d), lambda l, j: (0, 0)),
            pl.BlockSpec((1, d, tn), lambda l, j: (l, 0, j)),
            pl.BlockSpec((1, 1, tn), lambda l, j: (l, 0, j)),
        ],
        out_specs=pl.BlockSpec((1, rows, tn), lambda l, j: (l, 0, j)),
        compiler_params=_cparams(("parallel", "parallel")),
        name="modulation",
    )(cvec, w_mod, b_mod.reshape(depth, 1, n))


def _mod_rows(ml_ref, mc_ref, idx, is_ctx):
    return jnp.where(is_ctx, mc_ref[idx:idx + 1, :], ml_ref[0, idx:idx + 1, :])


def _modproj_kernel(x_ref, g_ref, ml_ref, mc_ref, *rest, starts, shift_idx, scale_idx, blocks_per_seq, ctx_len):
    ngroups = len(starts) - 1
    w_refs, o_refs, h_ref = rest[:ngroups], rest[ngroups:2 * ngroups], rest[2 * ngroups]
    i = pl.program_id(0)
    j = pl.program_id(1)

    @pl.when(j == 0)
    def _():
        tm = x_ref.shape[0]
        xn = _rms(x_ref[...], g_ref[...])
        row = (i % blocks_per_seq) * tm + lax.broadcasted_iota(jnp.int32, (tm, 1), 0)
        is_ctx = row < ctx_len
        scale = _mod_rows(ml_ref, mc_ref, scale_idx, is_ctx)
        shift = _mod_rows(ml_ref, mc_ref, shift_idx, is_ctx)
        h_ref[...] = (xn * (1.0 + scale) + shift).astype(BF)

    for k in range(ngroups):
        @pl.when(jnp.logical_and(j >= starts[k], j < starts[k + 1]))
        def _(k=k):
            o_refs[k][...] = jnp.dot(h_ref[...], w_refs[k][...], preferred_element_type=F32).astype(o_refs[k].dtype)


def _modproj(x2, gain, modl, modc, weights, tiles, *, seq_len, ctx_len, shift_idx, scale_idx):
    m, d = x2.shape
    tm = max(t for t in range(LANES, PROJ_ROWS + 1, LANES) if seq_len % t == 0)
    bps = seq_len // tm
    nblk = [w.shape[1] // tn for w, tn in zip(weights, tiles)]
    starts = [0]
    for nb in nblk:
        starts.append(starts[-1] + nb)

    def col(k):
        return lambda j: jnp.clip(j - starts[k], 0, nblk[k] - 1)

    kern = functools.partial(_modproj_kernel, starts=tuple(starts), shift_idx=shift_idx, scale_idx=scale_idx,
                             blocks_per_seq=bps, ctx_len=ctx_len)
    w_specs = [pl.BlockSpec((d, tn), lambda i, j, c=col(k): (0, c(j))) for k, tn in enumerate(tiles)]
    o_specs = [pl.BlockSpec((tm, tn), lambda i, j, c=col(k): (i, c(j))) for k, tn in enumerate(tiles)]
    return pl.pallas_call(
        kern,
        out_shape=[jax.ShapeDtypeStruct((m, w.shape[1]), BF) for w in weights],
        grid=(m // tm, starts[-1]),
        in_specs=[
            pl.BlockSpec((tm, d), lambda i, j: (i, 0)),
            pl.BlockSpec((1, d), lambda i, j: (0, 0)),
            pl.BlockSpec((1, SUBLANES, d), lambda i, j: (i // bps, 0, 0)),
            pl.BlockSpec((SUBLANES, d), lambda i, j: (0, 0)),
        ] + w_specs,
        out_specs=o_specs,
        scratch_shapes=[pltpu.VMEM((tm, d), BF)],
        compiler_params=_cparams(("parallel", "arbitrary")),
        name="modproj",
    )(x2, gain, modl, modc, *weights)


def _rwkv_prep_kernel(z_ref, zp_ref, zn_ref, mu_ref, vec_ref, w2_ref, a2_ref, g2_ref, ones_ref, trif_ref, trib_ref,
                      v_ref, pf_ref, qf_ref, kf_ref, rf_ref, pb_ref, qb_ref, kb_ref, rb_ref, wc_ref, g_ref, bon_ref,
                      *, nblocks):
    i = pl.program_id(1)
    z = z_ref[0].astype(F32)
    tb = z.shape[0]
    row = lax.broadcasted_iota(jnp.int32, (tb, 1), 0)
    prev_ok = i >= 2
    next_ok = jnp.logical_and(i >= 1, i <= nblocks - 2)
    zp_row = jnp.where(prev_ok, zp_ref[0, HALO - 1:HALO, :].astype(F32), 0.0)
    zn_row = jnp.where(next_ok, zn_ref[0, 0:1, :].astype(F32), 0.0)
    zprev = jnp.where(row == 0, zp_row, pltpu.roll(z, 1, 0))
    znext = jnp.where(row == tb - 1, zn_row, pltpu.roll(z, tb - 1, 0))
    zs = z + mu_ref[...] * (0.5 * (zprev + znext) - z)

    d = D_MODEL
    r = zs[:, 0:d]
    k = zs[:, d:2 * d]
    v = zs[:, 2 * d:3 * d]
    w_lo = _dot(jnp.tanh(zs[:, 3 * d:3 * d + 128]), w2_ref[...])
    a_lo = _dot(zs[:, 3 * d + 128:3 * d + 256], a2_ref[...])
    g_ref[0] = _dot(_sigmoid(zs[:, 3 * d + 256:3 * d + 384]), g2_ref[...])

    vec = vec_ref[...]
    ones = ones_ref[...]
    kk0 = k * vec[4:5]
    kk = kk0 * jnp.minimum(lax.rsqrt(_head_sums(kk0 * kk0, ones, True)), 1e12)

    keysum = None
    outs = ((pf_ref, qf_ref, kf_ref, rf_ref, trif_ref), (pb_ref, qb_ref, kb_ref, rb_ref, trib_ref))
    for dr, (p_ref, q_ref, k_ref, r_ref, tri_ref) in enumerate(outs):
        w_raw = vec[dr:dr + 1] + w_lo[:, dr * d:(dr + 1) * d]
        lw = -EXP_NEG_HALF * _sigmoid(w_raw)
        a = _sigmoid(vec[2 + dr:3 + dr] + a_lo[:, dr * d:(dr + 1) * d])
        key = k * (1.0 + (a - 1.0) * vec[5:6])
        cum = _split_dot_left(tri_ref[...], lw)
        wt = jnp.exp(cum)
        iw = jnp.exp(-cum)
        p_ref[0] = (kk * jnp.exp(cum - lw)).astype(BF)
        q_ref[0] = (a * kk * iw).astype(BF)
        k_ref[0] = (key * iw).astype(BF)
        r_ref[0] = (r * wt).astype(BF)
        for c in range(tb // CHUNK):
            last = c * CHUNK + (CHUNK - 1 if dr == 0 else 0)
            wc_ref[0, 0, dr * 4 + c:dr * 4 + c + 1, :] = wt[last:last + 1, :]
        keysum = key if keysum is None else keysum + key

    bon_ref[0] = _head_sums(r * keysum * vec[6:7], ones, False) * v
    v_ref[0] = v.astype(BF)


def _split_dot_left(w, x):
    hi = x.astype(BF)
    lo = (x - hi.astype(F32)).astype(BF)
    return jnp.dot(w, hi, preferred_element_type=F32) + jnp.dot(w, lo, preferred_element_type=F32)


def _rwkv_prep(z_rw, mu, vec, w2cat, a2cat, g2, ones, trif, trib):
    b, l, cols = z_rw.shape
    tb = TOKEN_BLOCK
    nb = l // tb
    d = D_MODEL
    hb = tb // HALO
    nhalo = l // HALO
    tok = lambda bi, i: (bi, i, 0)
    const2 = lambda bi, i: (0, 0)
    seq_bf = jax.ShapeDtypeStruct((b, l, d), BF)
    seq_f32 = jax.ShapeDtypeStruct((b, l, d), F32)
    out_tok = pl.BlockSpec((1, tb, d), tok)
    kern = functools.partial(_rwkv_prep_kernel, nblocks=nb)
    return pl.pallas_call(
        kern,
        out_shape=[seq_bf] * 9 + [jax.ShapeDtypeStruct((b, nb, SUBLANES, d), F32), seq_f32, seq_f32],
        grid=(b, nb),
        in_specs=[
            pl.BlockSpec((1, tb, cols), tok),
            pl.BlockSpec((1, HALO, cols), lambda bi, i: (bi, jnp.maximum(i * hb - 1, 0), 0)),
            pl.BlockSpec((1, HALO, cols), lambda bi, i: (bi, jnp.minimum((i + 1) * hb, nhalo - 1), 0)),
            pl.BlockSpec((1, cols), const2),
            pl.BlockSpec((SUBLANES, d), const2),
            pl.BlockSpec((LANES, 2 * d), const2),
            pl.BlockSpec((LANES, 2 * d), const2),
            pl.BlockSpec((LANES, d), const2),
            pl.BlockSpec((ONES_TILE, ONES_TILE), const2),
            pl.BlockSpec((tb, tb), const2),
            pl.BlockSpec((tb, tb), const2),
        ],
        out_specs=[out_tok] * 9 + [pl.BlockSpec((1, 1, SUBLANES, d), lambda bi, i: (bi, i, 0, 0)), out_tok, out_tok],
        compiler_params=_cparams(("parallel", "parallel")),
        name="rwkv_prep",
    )(z_rw, z_rw, z_rw, mu, vec, w2cat, a2cat, g2, ones, trif, trib)


def _sm(x, m_a):
    return jnp.concatenate([jnp.where(m_a, x, 0), jnp.where(m_a, 0, x)], axis=0).astype(BF)


def _chunk_step(chains, blk, eye, m_a, bd):
    c = CHUNK
    sm = lambda x: _sm(x, m_a)
    pm = lambda x, y: jnp.dot(x.astype(BF), sm(y), preferred_element_type=F32)
    each = lambda fn, *cols: [fn(*args) for args in zip(*cols)]

    strict = [ch["strict"] for ch in chains]
    incl = [ch["incl"] for ch in chains]
    p2, q2, k2, r2, v2, s2 = ([ch[n] for ch in chains] for n in ("p", "q", "k", "r", "v", "s"))
    pr = each(lambda p, r: jnp.concatenate([p, r], axis=0), p2, r2)
    gram = each(lambda a, q, k: _dot_nt(a, jnp.concatenate([sm(q), sm(k)], axis=0)), pr, q2, k2)
    prs = each(_dot_nt, pr, s2)
    lm = each(lambda g, m: jnp.where(m, g[:c, :2 * c], 0.0), gram, strict)
    rhs = each(lambda g, m, v, ps: ps[:c] + pm(jnp.where(m, g[:c, 2 * c:], 0.0), v), gram, strict, v2, prs)
    dm = each(lambda l: jnp.where(blk, l, 0.0), lm)
    em = each(lambda l, d: l - d, lm, dm)
    d2 = each(pm, dm, dm)
    n2 = each(lambda d, dd: (eye - d) + pm(eye - d, dd), dm, d2)
    d4 = each(pm, d2, d2)
    n3 = each(lambda n, dd: n + pm(n, dd), n2, d4)
    d8 = each(pm, d4, d4)
    td = each(lambda n, dd: n + pm(n, dd), n3, d8)
    f = each(pm, td, em)
    x1 = each(pm, td, rhs)
    f2 = each(pm, f, f)
    zz = each(lambda x, ff: x + pm(ff, x), x1, f2)
    u = each(lambda ff, z: pm(ff, z) - z, f, zz)
    y = each(lambda g, m, ps, uu, v: ps[c:] + _dot(
        jnp.concatenate([jnp.where(m, g[c:, :2 * c], 0.0), jnp.where(m, g[c:, 2 * c:], 0.0)], axis=1),
        jnp.concatenate([sm(uu), sm(v)], axis=0)), gram, incl, prs, u, v2)
    upd = each(lambda uu, v, q, k: _dot_tn(jnp.concatenate([uu.astype(BF), v], axis=0), jnp.concatenate([q, k], axis=0)),
               u, v2, q2, k2)
    s_new = each(lambda s, up, ch: jnp.where(bd, s + up, 0.0) * ch["w"], s2, upd, chains)
    return y, s_new


def _rwkv_scan_kernel(pf_ref, qf_ref, kf_ref, rf_ref, vf_ref, wcf_ref, pb_ref, qb_ref, kb_ref, rb_ref, vb_ref, wcb_ref,
                      yf_ref, yb_ref, sf_ref, sb_ref, *, bwd_chunk):
    n = pl.program_id(1)
    c = CHUNK
    npairs = sf_ref.shape[0]

    @pl.when(n == 0)
    def _():
        sf_ref[...] = jnp.zeros_like(sf_ref)
        sb_ref[...] = jnp.zeros_like(sb_ref)

    t_i = lax.broadcasted_iota(jnp.int32, (c, 2 * c), 0)
    s_i = jnp.bitwise_and(lax.broadcasted_iota(jnp.int32, (c, 2 * c), 1), c - 1)
    m_a = lax.broadcasted_iota(jnp.int32, (1, LANES), 1) < RW_HEAD
    blk = (t_i // 16) == (s_i // 16)
    eye = (t_i == s_i).astype(F32)
    bd = (lax.broadcasted_iota(jnp.int32, (LANES, LANES), 0) // RW_HEAD) == \
         (lax.broadcasted_iota(jnp.int32, (LANES, LANES), 1) // RW_HEAD)
    per_blk = TOKEN_BLOCK // c
    wrow_f = wcf_ref[0, 0, pl.ds(n % per_blk, 1), :]
    wrow_b = wcb_ref[0, 0, pl.ds(per_blk + bwd_chunk(n) % per_blk, 1), :]

    dirs = (
        (pf_ref, qf_ref, kf_ref, rf_ref, vf_ref, wrow_f, yf_ref, sf_ref, s_i < t_i, s_i <= t_i),
        (pb_ref, qb_ref, kb_ref, rb_ref, vb_ref, wrow_b, yb_ref, sb_ref, s_i > t_i, s_i >= t_i),
    )
    chains, sinks = [], []
    for p_ref, q_ref, k_ref, r_ref, v_ref, wrow, y_ref, s_ref, strict, incl in dirs:
        for p in range(npairs):
            sl = slice(p * LANES, (p + 1) * LANES)
            chains.append(dict(p=p_ref[0, :, sl], q=q_ref[0, :, sl], k=k_ref[0, :, sl], r=r_ref[0, :, sl],
                               v=v_ref[0, :, sl], s=s_ref[p], w=wrow[:, sl], strict=strict, incl=incl))
            sinks.append((y_ref, s_ref, p, sl))
    ys, s_news = _chunk_step(chains, blk, eye, m_a, bd)
    for (y_ref, s_ref, p, sl), y, s_new in zip(sinks, ys, s_news):
        y_ref[0, :, sl] = y
        s_ref[p] = s_new


def _rwkv_scan(v, pf, qf, kf, rf, pb, qb, kb, rb, wc, *, ctx_len):
    b, l, d = v.shape
    c = CHUNK
    nch = l // c
    nctx = ctx_len // c
    per_blk = TOKEN_BLOCK // c

    def bwd_chunk(n):
        return jnp.where(n < nctx, nctx - 1 - n, nch + nctx - 1 - n)

    fwd = pl.BlockSpec((1, c, d), lambda bi, n: (bi, n, 0))
    bwd = pl.BlockSpec((1, c, d), lambda bi, n: (bi, bwd_chunk(n), 0))
    wcf = pl.BlockSpec((1, 1, SUBLANES, d), lambda bi, n: (bi, n // per_blk, 0, 0))
    wcb = pl.BlockSpec((1, 1, SUBLANES, d), lambda bi, n: (bi, bwd_chunk(n) // per_blk, 0, 0))
    npairs = d // LANES
    kern = functools.partial(_rwkv_scan_kernel, bwd_chunk=bwd_chunk)
    return pl.pallas_call(
        kern,
        out_shape=[jax.ShapeDtypeStruct((b, l, d), F32)] * 2,
        grid=(b, nch),
        in_specs=[fwd] * 5 + [wcf] + [bwd] * 5 + [wcb],
        out_specs=[fwd, bwd],
        scratch_shapes=[pltpu.VMEM((npairs, LANES, LANES), F32)] * 2,
        compiler_params=_cparams(("parallel", "arbitrary")),
        name="rwkv_scan",
    )(pf, qf, kf, rf, v, wc, pb, qb, kb, rb, v, wc)


def _rwkv_readout_kernel(yf_ref, yb_ref, bon_ref, g_ref, vec_ref, ones_ref, o_ref):
    ones = ones_ref[...]
    y = yf_ref[...] + yb_ref[...]
    inv_n = 1.0 / RW_HEAD
    yc = y - _head_sums(y, ones, True) * inv_n
    var = _head_sums(yc * yc, ones, True) * inv_n
    yn = yc * lax.rsqrt(var + RW_GN_EPS) * vec_ref[0:1, :] + vec_ref[1:2, :]
    o_ref[...] = ((yn + bon_ref[...]) * g_ref[...]).astype(BF)


def _rwkv_readout(yf, yb, bon, g, vec, ones):
    m, d = yf.shape
    tm = TOKEN_BLOCK
    tok = pl.BlockSpec((tm, d), lambda i: (i, 0))
    return pl.pallas_call(
        _rwkv_readout_kernel,
        out_shape=jax.ShapeDtypeStruct((m, d), BF),
        grid=(m // tm,),
        in_specs=[tok, tok, tok, tok, pl.BlockSpec((SUBLANES, d), lambda i: (0, 0)),
                  pl.BlockSpec((ONES_TILE, ONES_TILE), lambda i: (0, 0))],
        out_specs=tok,
        compiler_params=_cparams(("parallel",)),
        name="rwkv_readout",
    )(yf, yb, bon, g, vec, ones)


CONV_ROWS = 32
CONV_LANES = 512


def _conv_kernel(z_ref, zp_ref, zn_ref, w_ref, vec_ref, o_ref, gp_ref, gs_ref, u_ref, *, nblocks):
    i = pl.program_id(1)
    tb = z_ref.shape[1]
    d = D_MODEL
    h = HALO

    def glu(zz):
        zz = zz.astype(F32)
        return zz[:, :d] * _sigmoid(zz[:, d:])

    prev_ok = i >= 2
    next_ok = jnp.logical_and(i >= 1, i <= nblocks - 2)
    gp_ref[0:h, :] = jnp.where(prev_ok, glu(zp_ref[0]), 0.0)
    gp_ref[h:h + tb, :] = glu(z_ref[0])
    gp_ref[h + tb:2 * h + tb, :] = jnp.where(next_ok, glu(zn_ref[0]), 0.0)
    rows = gs_ref.shape[1]
    for s in range(SUBLANES):
        gs_ref[s] = gp_ref[s:s + rows, :]

    off = h - CONV_WIDTH // 2
    grp = CONV_ROWS // SUBLANES
    for r0 in range(0, tb, CONV_ROWS):
        for c0 in range(0, d, CONV_LANES):
            acc = jnp.zeros((grp, SUBLANES, CONV_LANES), F32)
            for k in range(CONV_WIDTH):
                a, s = divmod(k + off, SUBLANES)
                win = gs_ref[s, r0 + a * SUBLANES:r0 + a * SUBLANES + CONV_ROWS, c0:c0 + CONV_LANES]
                acc = acc + win.reshape(grp, SUBLANES, CONV_LANES) * w_ref[k, :, c0:c0 + CONV_LANES][None]
            u_ref[r0:r0 + CONV_ROWS, c0:c0 + CONV_LANES] = acc.reshape(CONV_ROWS, CONV_LANES)

    u = u_ref[...] + vec_ref[0:1, :]
    mu = jnp.mean(u, axis=-1, keepdims=True)
    uc = u - mu
    var = jnp.mean(uc * uc, axis=-1, keepdims=True)
    y = uc * lax.rsqrt(var + EPS) * vec_ref[1:2, :] + vec_ref[2:3, :]
    o_ref[0] = (y * _sigmoid(y)).astype(BF)


def _conv_branch(z_cv, w, vec):
    b, l, cols = z_cv.shape
    tb = TOKEN_BLOCK
    nb = l // tb
    d = D_MODEL
    hb = tb // HALO
    nhalo = l // HALO
    shifted_rows = tb + 2 * HALO - SUBLANES
    kern = functools.partial(_conv_kernel, nblocks=nb)
    return pl.pallas_call(
        kern,
        out_shape=jax.ShapeDtypeStruct((b, l, d), BF),
        grid=(b, nb),
        in_specs=[
            pl.BlockSpec((1, tb, cols), lambda bi, i: (bi, i, 0)),
            pl.BlockSpec((1, HALO, cols), lambda bi, i: (bi, jnp.maximum(i * hb - 1, 0), 0)),
            pl.BlockSpec((1, HALO, cols), lambda bi, i: (bi, jnp.minimum((i + 1) * hb, nhalo - 1), 0)),
            pl.BlockSpec((CONV_WIDTH, SUBLANES, d), lambda bi, i: (0, 0, 0)),
            pl.BlockSpec((SUBLANES, d), lambda bi, i: (0, 0)),
        ],
        out_specs=pl.BlockSpec((1, tb, d), lambda bi, i: (bi, i, 0)),
        scratch_shapes=[pltpu.VMEM((tb + 2 * HALO, d), F32), pltpu.VMEM((SUBLANES, shifted_rows, d), F32),
                        pltpu.VMEM((tb, d), F32)],
        compiler_params=_cparams(("parallel", "parallel")),
        name="conv_branch",
    )(z_cv, z_cv, z_cv, w, vec)


def _att_prep_kernel(z_ref, cos_ref, sin_ref, gq_ref, gk_ref, q_ref, k_ref, v_ref):
    cos = cos_ref[...]
    sin = sin_ref[...]
    hd = ATT_HEAD

    def norm_rope(x, g):
        xn = _rms(x.astype(F32), g)
        return xn * cos + pltpu.roll(xn, hd // 2, 1) * sin

    q_scale = (hd ** -0.5) * LOG2_E
    for h in range(ATT_Q_HEADS):
        sl = slice(h * hd, (h + 1) * hd)
        q_ref[0, :, sl] = (norm_rope(z_ref[0, :, sl], gq_ref[...]) * q_scale).astype(BF)
    for h in range(ATT_KV_HEADS):
        sl = slice(h * hd, (h + 1) * hd)
        k_ref[0, :, sl] = norm_rope(z_ref[0, :, D_MODEL + h * hd:D_MODEL + (h + 1) * hd], gk_ref[...]).astype(BF)
        v_ref[0, :, 2 * h * hd:(2 * h + 1) * hd] = z_ref[0, :, D_MODEL + KV_COLS + h * hd:D_MODEL + KV_COLS + (h + 1) * hd]
        v_ref[0, :, (2 * h + 1) * hd:(2 * h + 2) * hd] = jnp.ones((z_ref.shape[1], hd), BF)


def _att_prep(z_at, cos_t, sin_t, gq, gk):
    b, l, cols = z_at.shape
    tb = TOKEN_BLOCK
    tok = lambda bi, i: (bi, i, 0)
    return pl.pallas_call(
        _att_prep_kernel,
        out_shape=[jax.ShapeDtypeStruct((b, l, D_MODEL), BF), jax.ShapeDtypeStruct((b, l, KV_COLS), BF),
                   jax.ShapeDtypeStruct((b, l, 2 * KV_COLS), BF)],
        grid=(b, l // tb),
        in_specs=[
            pl.BlockSpec((1, tb, cols), tok),
            pl.BlockSpec((tb, ATT_HEAD), lambda bi, i: (i, 0)),
            pl.BlockSpec((tb, ATT_HEAD), lambda bi, i: (i, 0)),
            pl.BlockSpec((1, ATT_HEAD), lambda bi, i: (0, 0)),
            pl.BlockSpec((1, ATT_HEAD), lambda bi, i: (0, 0)),
        ],
        out_specs=[pl.BlockSpec((1, tb, D_MODEL), tok), pl.BlockSpec((1, tb, KV_COLS), tok),
                   pl.BlockSpec((1, tb, 2 * KV_COLS), tok)],
        compiler_params=_cparams(("parallel", "parallel")),
        name="att_prep",
    )(z_at, cos_t, sin_t, gq, gk)


def _attn_kernel(q_ref, k_ref, v_ref, o_ref, *, ctx_len):
    hd = ATT_HEAD

    def run(nk):
        s_prev = p_prev = None
        tq = q_ref.shape[1]
        nunits = ATT_Q_HEADS // ATT_STACK
        for un in range(nunits + 2):
            s_new = None
            if un < nunits:
                hk = un * ATT_STACK // ATT_GROUP
                qs = [q_ref[0, :, h * hd:(h + 1) * hd] for h in range(un * ATT_STACK, (un + 1) * ATT_STACK)]
                s_new = _dot_nt(qs[0] if ATT_STACK == 1 else jnp.concatenate(qs, axis=0), k_ref[0, 0:nk, hk * hd:(hk + 1) * hd])
            p_new = None
            if s_prev is not None:
                p_new = jnp.exp2(s_prev - jnp.max(s_prev, axis=-1, keepdims=True)).astype(BF)
            if p_prev is not None:
                uo = un - 2
                hk = uo * ATT_STACK // ATT_GROUP
                o = jnp.dot(p_prev, v_ref[0, 0:nk, 2 * hk * hd:(2 * hk + 2) * hd], preferred_element_type=F32)
                o = (o[:, :hd] / o[:, hd:]).astype(BF)
                for g in range(ATT_STACK):
                    ho = uo * ATT_STACK + g
                    o_ref[0, :, ho * hd:(ho + 1) * hd] = o[g * tq:(g + 1) * tq]
            s_prev, p_prev = s_new, p_new

    @pl.when(pl.program_id(1) == 0)
    def _():
        run(ctx_len)

    @pl.when(pl.program_id(1) > 0)
    def _():
        run(k_ref.shape[1])


def _attention(q, k, v, *, ctx_len):
    b, l, d = q.shape
    tb = TOKEN_BLOCK
    kern = functools.partial(_attn_kernel, ctx_len=ctx_len)
    seq = lambda bi, i: (bi, 0, 0)
    return pl.pallas_call(
        kern,
        out_shape=jax.ShapeDtypeStruct((b, l, d), BF),
        grid=(b, l // tb),
        in_specs=[pl.BlockSpec((1, tb, d), lambda bi, i: (bi, i, 0)), pl.BlockSpec((1, l, KV_COLS), seq),
                  pl.BlockSpec((1, l, 2 * KV_COLS), seq)],
        out_specs=pl.BlockSpec((1, tb, d), lambda bi, i: (bi, i, 0)),
        compiler_params=_cparams(("parallel", "parallel")),
        name="attention",
    )(q, k, v)


def _merge_kernel(orw_ref, ocv_ref, oat_ref, zg_ref, x_ref, wb_ref, wo_ref, g_ref, ml_ref, mc_ref, o_ref, *, blocks_per_seq):
    d = D_MODEL
    is_ctx = (pl.program_id(0) % blocks_per_seq) == 0
    m = None
    for n, o_n in enumerate((orw_ref, ocv_ref, oat_ref)):
        t = _sigmoid(zg_ref[:, n * d:(n + 1) * d].astype(F32)) * jnp.dot(o_n[...], wb_ref[n], preferred_element_type=F32)
        m = t if m is None else m + t
    out = _dot(m, wo_ref[...])
    o_ref[...] = x_ref[...] + _mod_rows(ml_ref, mc_ref, 2, is_ctx) * _rms(out, g_ref[...])


def _merge(orw, ocv, oat, zg, x2, wb, wo, gain, modl, modc, *, seq_len):
    m, d = x2.shape
    tm = TOKEN_BLOCK
    bps = seq_len // tm
    tok = pl.BlockSpec((tm, d), lambda i: (i, 0))
    kern = functools.partial(_merge_kernel, blocks_per_seq=bps)
    return pl.pallas_call(
        kern,
        out_shape=jax.ShapeDtypeStruct((m, d), F32),
        grid=(m // tm,),
        in_specs=[
            tok, tok, tok,
            pl.BlockSpec((tm, 3 * d), lambda i: (i, 0)),
            tok,
            pl.BlockSpec((3, d, d), lambda i: (0, 0, 0)),
            pl.BlockSpec((d, d), lambda i: (0, 0)),
            pl.BlockSpec((1, d), lambda i: (0, 0)),
            pl.BlockSpec((1, SUBLANES, d), lambda i: (i // bps, 0, 0)),
            pl.BlockSpec((SUBLANES, d), lambda i: (0, 0)),
        ],
        out_specs=tok,
        compiler_params=_cparams(("parallel",)),
        name="merge",
    )(orw, ocv, oat, zg, x2, wb, wo, gain, modl, modc)


def _mlp_kernel(x_ref, w1_ref, w2_ref, gpre_ref, gpost_ref, ml_ref, mc_ref, o_ref, *, blocks_per_seq):
    is_ctx = (pl.program_id(0) % blocks_per_seq) == 0
    x = x_ref[...]
    h = _rms(x, gpre_ref[...]) * (1.0 + _mod_rows(ml_ref, mc_ref, 4, is_ctx)) + _mod_rows(ml_ref, mc_ref, 3, is_ctx)
    a = jnp.maximum(_dot(h, w1_ref[...]), 0.0)
    out = _dot(a * a, w2_ref[...])
    o_ref[...] = x + _mod_rows(ml_ref, mc_ref, 5, is_ctx) * _rms(out, gpost_ref[...])


def _mlp(x2, w1, w2, gpre, gpost, modl, modc, *, seq_len):
    m, d = x2.shape
    tm = TOKEN_BLOCK
    bps = seq_len // tm
    dff = w1.shape[1]
    tok = pl.BlockSpec((tm, d), lambda i: (i, 0))
    kern = functools.partial(_mlp_kernel, blocks_per_seq=bps)
    return pl.pallas_call(
        kern,
        out_shape=jax.ShapeDtypeStruct((m, d), F32),
        grid=(m // tm,),
        in_specs=[
            tok,
            pl.BlockSpec((d, dff), lambda i: (0, 0)),
            pl.BlockSpec((dff, d), lambda i: (0, 0)),
            pl.BlockSpec((1, d), lambda i: (0, 0)),
            pl.BlockSpec((1, d), lambda i: (0, 0)),
            pl.BlockSpec((1, SUBLANES, d), lambda i: (i // bps, 0, 0)),
            pl.BlockSpec((SUBLANES, d), lambda i: (0, 0)),
        ],
        out_specs=tok,
        compiler_params=_cparams(("parallel",)),
        name="mlp",
    )(x2, w1, w2, gpre, gpost, modl, modc)


def _pad_rows(a, rows):
    return jnp.pad(a, ((0, rows - a.shape[0]), (0, 0)))


def _rope_tables(seq, ctx_len):
    rows = seq // GRID_W
    row = jnp.repeat(jnp.arange(rows), GRID_W).astype(F32)
    col = jnp.tile(jnp.arange(GRID_W), rows).astype(F32)
    axis_dim = ATT_HEAD // 2
    freqs = ROPE_THETA ** (-jnp.arange(0, axis_dim, 2, dtype=F32) / axis_dim)
    ang = jnp.concatenate([row[:, None] * freqs, col[:, None] * freqs], axis=-1)
    cos, sin = jnp.cos(ang), jnp.sin(ang)
    cos_t = jnp.concatenate([cos, cos], axis=-1)
    sin_t = jnp.concatenate([-sin, sin], axis=-1)
    cos_t = jnp.concatenate([jnp.ones((ctx_len, ATT_HEAD), F32), cos_t], axis=0)
    sin_t = jnp.concatenate([jnp.zeros((ctx_len, ATT_HEAD), F32), sin_t], axis=0)
    return cos_t, sin_t


def _chunk_tri(upper):
    t = jnp.arange(TOKEN_BLOCK)
    same = (t[:, None] // CHUNK) == (t[None, :] // CHUNK)
    tri = (t[None, :] >= t[:, None]) if upper else (t[None, :] <= t[:, None])
    return jnp.logical_and(same, tri).astype(BF)


def kernel(x, c, ctx, c_ctx, w_mod, b_mod, norm_mix_pre, norm_mix_post, norm_mlp_pre, norm_mlp_post, w_in, rw_mu, rw_w0, rw_w2, rw_a0, rw_a2, rw_g2, rw_k_k, rw_k_a, rw_r_k, rw_ln_g, rw_ln_b, cv_dw_w, cv_dw_b, cv_ln_g, cv_ln_b, at_q_norm, at_k_norm, w_branch, w_out, w_ff1, w_ff2):
    b, s, d = x.shape
    ctx_len = ctx.shape[1]
    depth = w_in.shape[0]
    l = ctx_len + s
    assert d == D_MODEL and ctx_len == TOKEN_BLOCK and s % TOKEN_BLOCK == 0

    mod_rows = -(-(b + 1) // SUBLANES) * SUBLANES
    cvec = _pad_rows(jnp.concatenate([c, c_ctx[None, :]], axis=0), mod_rows)
    mods = _modulation(cvec, w_mod, b_mod).reshape(depth, mod_rows, N_MOD, d)
    mods = jnp.pad(mods, ((0, 0), (0, 0), (0, SUBLANES - N_MOD), (0, 0)))

    cos_t, sin_t = _rope_tables(s, ctx_len)
    half_perm = jnp.concatenate([jnp.arange(0, ATT_HEAD, 2), jnp.arange(1, ATT_HEAD, 2)])
    qk_perm = (jnp.arange(ATT_Q_HEADS + ATT_KV_HEADS)[:, None] * ATT_HEAD + half_perm[None, :]).reshape(-1)
    ones_blk = (jnp.arange(ONES_TILE)[:, None] // RW_HEAD == jnp.arange(ONES_TILE)[None, :] // RW_HEAD).astype(BF)
    trif, trib = _chunk_tri(False), _chunk_tri(True)
    zpad64 = jnp.zeros((64, d), F32)

    xu = jnp.concatenate([ctx, x], axis=1).reshape(b * l, d)

    for li in range(depth):
        modl, modc = mods[li, :b], mods[li, b]
        w_l = w_in[li]
        c0, c1, c2 = RWKV_COLS, RWKV_COLS + 2 * d, RWKV_COLS + 2 * d + ATT_COLS
        w_rw = w_l[:, :c0].astype(BF)
        w_cv = w_l[:, c0:c1].astype(BF)
        w_at = w_l[:, c1:c2]
        w_at = jnp.concatenate([w_at[:, qk_perm], w_at[:, d + KV_COLS:]], axis=1).astype(BF)
        w_gt = w_l[:, c2:].astype(BF)
        z_rw, z_cv, z_at, z_gt = _modproj(xu, norm_mix_pre[li][None, :], modl, modc, (w_rw, w_cv, w_at, w_gt),
                                          (1152, 1024, 768, 1024), seq_len=l, ctx_len=ctx_len, shift_idx=0, scale_idx=1)
        z_rw = z_rw.reshape(b, l, RWKV_COLS)
        z_cv = z_cv.reshape(b, l, 2 * d)
        z_at = z_at.reshape(b, l, ATT_COLS)

        rw_vec = _pad_rows(jnp.stack([rw_w0[li, 0], rw_w0[li, 1], rw_a0[li, 0], rw_a0[li, 1], rw_k_k[li], rw_k_a[li],
                                      rw_r_k[li].reshape(-1)]), SUBLANES)
        w2cat = jnp.concatenate([jnp.concatenate([rw_w2[li, 0], zpad64], axis=0),
                                 jnp.concatenate([zpad64, rw_w2[li, 1]], axis=0)], axis=1).astype(BF)
        a2cat = jnp.concatenate([jnp.concatenate([rw_a2[li, 0], zpad64], axis=0),
                                 jnp.concatenate([zpad64, rw_a2[li, 1]], axis=0)], axis=1).astype(BF)
        v, pf, qf, kf, rf, pb, qb, kb, rb, wc, g, bon = _rwkv_prep(
            z_rw, rw_mu[li][None, :], rw_vec, w2cat, a2cat, rw_g2[li].astype(BF), ones_blk, trif, trib)
        yf, yb = _rwkv_scan(v, pf, qf, kf, rf, pb, qb, kb, rb, wc, ctx_len=ctx_len)
        ln_vec = _pad_rows(jnp.stack([rw_ln_g[li], rw_ln_b[li]]), SUBLANES)
        o_rw = _rwkv_readout(yf.reshape(b * l, d), yb.reshape(b * l, d), bon.reshape(b * l, d), g.reshape(b * l, d),
                             ln_vec, ones_blk)

        cv_vec = _pad_rows(jnp.stack([cv_dw_b[li], cv_ln_g[li], cv_ln_b[li]]), SUBLANES)
        dw_w = jnp.broadcast_to(cv_dw_w[li][:, None, :], (CONV_WIDTH, SUBLANES, d))
        o_cv = _conv_branch(z_cv, dw_w, cv_vec).reshape(b * l, d)

        q, k, vv = _att_prep(z_at, cos_t, sin_t, at_q_norm[li][half_perm][None, :], at_k_norm[li][half_perm][None, :])
        o_at = _attention(q, k, vv, ctx_len=ctx_len).reshape(b * l, d)

        xu = _merge(o_rw, o_cv, o_at, z_gt, xu, w_branch[li].astype(BF), w_out[li].astype(BF),
                    norm_mix_post[li][None, :], modl, modc, seq_len=l)
        xu = _mlp(xu, w_ff1[li].astype(BF), w_ff2[li].astype(BF), norm_mlp_pre[li][None, :], norm_mlp_post[li][None, :],
                  modl, modc, seq_len=l)

    return xu.reshape(b, l, d)[:, ctx_len:, :]
```

```python
import functools
import math

import jax
import jax.numpy as jnp
from jax import lax
from jax.experimental import pallas as pl
from jax.experimental.pallas import tpu as pltpu

F32 = jnp.float32
BF = jnp.bfloat16

D_MODEL = 1024
N_MOD = 6
EPS = 1e-6
RW_HEAD = 64
RW_HEADS = 16
RW_GN_EPS = 64e-5
RW_LORA_COLS = 384
RWKV_COLS = 3 * D_MODEL + RW_LORA_COLS
CONV_WIDTH = 31
HALO = 16
ONES_TILE = 256
ATT_HEAD = 128
ATT_Q_HEADS = 8
ATT_KV_HEADS = 2
ATT_GROUP = ATT_Q_HEADS // ATT_KV_HEADS
ATT_STACK = 2
KV_COLS = ATT_KV_HEADS * ATT_HEAD
ATT_COLS = D_MODEL + 2 * KV_COLS
GRID_W = 64
ROPE_THETA = 10000.0
D_FF = 4 * D_MODEL

LANES = 128
SUBLANES = 8
CHUNK = 64
TOKEN_BLOCK = 256
PROJ_ROWS = 768
FF_TILE = 1024
VMEM_LIMIT = 56 * 1024 * 1024
EXP_NEG_HALF = math.exp(-0.5)
LOG2_E = math.log2(math.e)


def _cparams(sem):
    return pltpu.CompilerParams(dimension_semantics=sem, vmem_limit_bytes=VMEM_LIMIT)


def _dot(a, b):
    return jnp.dot(a.astype(BF), b.astype(BF), preferred_element_type=F32)


def _dot_nt(a, b):
    return lax.dot_general(a.astype(BF), b.astype(BF), (((1,), (1,)), ((), ())), preferred_element_type=F32)


def _dot_tn(a, b):
    return lax.dot_general(a.astype(BF), b.astype(BF), (((0,), (0,)), ((), ())), preferred_element_type=F32)


def _split_dot(x, w):
    hi = x.astype(BF)
    lo = (x - hi.astype(F32)).astype(BF)
    return jnp.dot(hi, w, preferred_element_type=F32) + jnp.dot(lo, w, preferred_element_type=F32)


def _head_sums(x, ones, split):
    w = ones.shape[0]
    dot = _split_dot if split else _dot
    return jnp.concatenate([dot(x[:, n * w:(n + 1) * w], ones) for n in range(x.shape[1] // w)], axis=1)


def _sigmoid(x):
    return 1.0 / (1.0 + jnp.exp(-x))


def _rms(x, g):
    return x * lax.rsqrt(jnp.mean(x * x, axis=-1, keepdims=True) + EPS) * g


def _mod_kernel(c_ref, w_ref, b_ref, o_ref):
    c = c_ref[...]
    o_ref[0] = _dot(c * _sigmoid(c), w_ref[0]) + b_ref[0]


def _modulation(cvec, w_mod, b_mod):
    depth, d, n = w_mod.shape
    rows = cvec.shape[0]
    tn = 1536
    return pl.pallas_call(
        _mod_kernel,
        out_shape=jax.ShapeDtypeStruct((depth, rows, n), F32),
        grid=(depth, n // tn),
        in_specs=[
            pl.BlockSpec((rows, d), lambda l, j: (0, 0)),
            pl.BlockSpec((1, d, tn), lambda l, j: (l, 0, j)),
            pl.BlockSpec((1, 1, tn), lambda l, j: (l, 0, j)),
        ],
        out_specs=pl.BlockSpec((1, rows, tn), lambda l, j: (l, 0, j)),
        compiler_params=_cparams(("parallel", "parallel")),
        name="modulation",
    )(cvec, w_mod, b_mod.reshape(depth, 1, n))


def _mod_rows(ml_ref, mc_ref, idx, is_ctx):
    return jnp.where(is_ctx, mc_ref[idx:idx + 1, :], ml_ref[0, idx:idx + 1, :])


def _ctx_rows(tm, blocks_per_seq, ctx_len):
    row = (pl.program_id(0) % blocks_per_seq) * tm + lax.broadcasted_iota(jnp.int32, (tm, 1), 0)
    return row < ctx_len


def _row_block(seq_len):
    return max(t for t in range(LANES, PROJ_ROWS + 1, LANES) if seq_len % t == 0)


def _resident(shape):
    return pl.BlockSpec(shape, lambda *_: (0,) * len(shape), pipeline_mode=pl.Buffered(1))


def _modproj_kernel(x_ref, g_ref, ml_ref, mc_ref, *rest, starts, shift_idx, scale_idx, blocks_per_seq, ctx_len):
    ngroups = len(starts) - 1
    w_refs, o_refs, h_ref = rest[:ngroups], rest[ngroups:2 * ngroups], rest[2 * ngroups]
    j = pl.program_id(1)

    @pl.when(j == 0)
    def _():
        xn = _rms(x_ref[...], g_ref[...])
        is_ctx = _ctx_rows(x_ref.shape[0], blocks_per_seq, ctx_len)
        scale = _mod_rows(ml_ref, mc_ref, scale_idx, is_ctx)
        shift = _mod_rows(ml_ref, mc_ref, shift_idx, is_ctx)
        h_ref[...] = (xn * (1.0 + scale) + shift).astype(BF)

    for k in range(ngroups):
        tn = o_refs[k].shape[1]
        for t in range(starts[k + 1] - starts[k]):
            @pl.when(j == starts[k] + t)
            def _(k=k, t=t, tn=tn):
                o_refs[k][...] = jnp.dot(h_ref[...], w_refs[k][:, t * tn:(t + 1) * tn],
                                         preferred_element_type=F32).astype(o_refs[k].dtype)


def _modproj(x2, gain, modl, modc, weights, tiles, *, seq_len, ctx_len, shift_idx, scale_idx):
    m, d = x2.shape
    tm = _row_block(seq_len)
    bps = seq_len // tm
    nblk = [w.shape[1] // tn for w, tn in zip(weights, tiles)]
    starts = [0]
    for nb in nblk:
        starts.append(starts[-1] + nb)

    def col(k):
        return lambda j: jnp.clip(j - starts[k], 0, nblk[k] - 1)

    kern = functools.partial(_modproj_kernel, starts=tuple(starts), shift_idx=shift_idx, scale_idx=scale_idx,
                             blocks_per_seq=bps, ctx_len=ctx_len)
    w_specs = [_resident(w.shape) for w in weights]
    o_specs = [pl.BlockSpec((tm, tn), lambda i, j, c=col(k): (i, c(j))) for k, tn in enumerate(tiles)]
    return pl.pallas_call(
        kern,
        out_shape=[jax.ShapeDtypeStruct((m, w.shape[1]), BF) for w in weights],
        grid=(m // tm, starts[-1]),
        in_specs=[
            pl.BlockSpec((tm, d), lambda i, j: (i, 0)),
            pl.BlockSpec((1, d), lambda i, j: (0, 0)),
            pl.BlockSpec((1, SUBLANES, d), lambda i, j: (i // bps, 0, 0)),
            pl.BlockSpec((SUBLANES, d), lambda i, j: (0, 0)),
        ] + w_specs,
        out_specs=o_specs,
        scratch_shapes=[pltpu.VMEM((tm, d), BF)],
        compiler_params=_cparams(("parallel", "arbitrary")),
        name="modproj",
    )(x2, gain, modl, modc, *weights)


def _rwkv_prep_kernel(z_ref, zp_ref, zn_ref, mu_ref, vec_ref, w2_ref, a2_ref, g2_ref, ones_ref, trif_ref, trib_ref, band_ref,
                      v_ref, pf_ref, qf_ref, kf_ref, rf_ref, pb_ref, qb_ref, kb_ref, rb_ref, wc_ref, g_ref, bon_ref,
                      *, nblocks):
    i = pl.program_id(1)
    zb = z_ref[0]
    z = zb.astype(F32)
    tb = z.shape[0]
    prev_ok = i >= 2
    next_ok = jnp.logical_and(i >= 1, i <= nblocks - 2)
    zp_row = jnp.where(prev_ok, zp_ref[0, HALO - 1:HALO, :].astype(F32), 0.0)
    zn_row = jnp.where(next_ok, zn_ref[0, 0:1, :].astype(F32), 0.0)
    delta = jnp.dot(band_ref[...], zb, preferred_element_type=F32)
    sub = lax.broadcasted_iota(jnp.int32, (SUBLANES, 1), 0)
    top = delta[0:SUBLANES] + jnp.where(sub == 0, 0.5 * zp_row, 0.0)
    bot = delta[tb - SUBLANES:tb] + jnp.where(sub == SUBLANES - 1, 0.5 * zn_row, 0.0)
    delta = jnp.concatenate([top, delta[SUBLANES:tb - SUBLANES], bot], axis=0)
    zs = z + mu_ref[...] * delta

    d = D_MODEL
    r = zs[:, 0:d]
    k = zs[:, d:2 * d]
    v = zs[:, 2 * d:3 * d]
    w_lo = _dot(jnp.tanh(zs[:, 3 * d:3 * d + 128]), w2_ref[...])
    a_lo = _dot(zs[:, 3 * d + 128:3 * d + 256], a2_ref[...])
    g_ref[0] = _dot(_sigmoid(zs[:, 3 * d + 256:3 * d + 384]), g2_ref[...]).astype(BF)

    vec = vec_ref[...]
    ones = ones_ref[...]
    kk0 = k * vec[4:5]
    kk = kk0 * jnp.minimum(lax.rsqrt(_head_sums(kk0 * kk0, ones, True)), 1e12)

    keysum = None
    outs = ((pf_ref, qf_ref, kf_ref, rf_ref, trif_ref), (pb_ref, qb_ref, kb_ref, rb_ref, trib_ref))
    for dr, (p_ref, q_ref, k_ref, r_ref, tri_ref) in enumerate(outs):
        w_raw = vec[dr:dr + 1] + w_lo[:, dr * d:(dr + 1) * d]
        lw = -EXP_NEG_HALF * _sigmoid(w_raw)
        a = _sigmoid(vec[2 + dr:3 + dr] + a_lo[:, dr * d:(dr + 1) * d])
        key = k * (1.0 + (a - 1.0) * vec[5:6])
        cum = _split_dot_left(tri_ref[...], lw)
        wt = jnp.exp(cum)
        iw = jnp.exp(-cum)
        p_ref[0] = (kk * jnp.exp(cum - lw)).astype(BF)
        q_ref[0] = (a * kk * iw).astype(BF)
        k_ref[0] = (key * iw).astype(BF)
        r_ref[0] = (r * wt).astype(BF)
        for c in range(tb // CHUNK):
            last = c * CHUNK + (CHUNK - 1 if dr == 0 else 0)
            wc_ref[0, 0, dr * 4 + c:dr * 4 + c + 1, :] = wt[last:last + 1, :]
        keysum = key if keysum is None else keysum + key

    bon_ref[0] = (_head_sums(r * keysum * vec[6:7], ones, False) * v).astype(BF)
    v_ref[0] = v.astype(BF)


def _split_dot_left(w, x):
    hi = x.astype(BF)
    lo = (x - hi.astype(F32)).astype(BF)
    return jnp.dot(w, hi, preferred_element_type=F32) + jnp.dot(w, lo, preferred_element_type=F32)


def _rwkv_prep(z_rw, mu, vec, w2cat, a2cat, g2, ones, trif, trib, band):
    b, l, cols = z_rw.shape
    tb = TOKEN_BLOCK
    nb = l // tb
    d = D_MODEL
    hb = tb // HALO
    nhalo = l // HALO
    tok = lambda bi, i: (bi, i, 0)
    const2 = lambda bi, i: (0, 0)
    seq_bf = jax.ShapeDtypeStruct((b, l, d), BF)
    out_tok = pl.BlockSpec((1, tb, d), tok)
    kern = functools.partial(_rwkv_prep_kernel, nblocks=nb)
    return pl.pallas_call(
        kern,
        out_shape=[seq_bf] * 9 + [jax.ShapeDtypeStruct((b, nb, SUBLANES, d), F32), seq_bf, seq_bf],
        grid=(b, nb),
        in_specs=[
            pl.BlockSpec((1, tb, cols), tok),
            pl.BlockSpec((1, HALO, cols), lambda bi, i: (bi, jnp.maximum(i * hb - 1, 0), 0)),
            pl.BlockSpec((1, HALO, cols), lambda bi, i: (bi, jnp.minimum((i + 1) * hb, nhalo - 1), 0)),
            pl.BlockSpec((1, cols), const2),
            pl.BlockSpec((SUBLANES, d), const2),
            pl.BlockSpec((LANES, 2 * d), const2),
            pl.BlockSpec((LANES, 2 * d), const2),
            pl.BlockSpec((LANES, d), const2),
            pl.BlockSpec((ONES_TILE, ONES_TILE), const2),
            pl.BlockSpec((tb, tb), const2),
            pl.BlockSpec((tb, tb), const2),
            pl.BlockSpec((tb, tb), const2),
        ],
        out_specs=[out_tok] * 9 + [pl.BlockSpec((1, 1, SUBLANES, d), lambda bi, i: (bi, i, 0, 0)), out_tok, out_tok],
        compiler_params=_cparams(("parallel", "parallel")),
        name="rwkv_prep",
    )(z_rw, z_rw, z_rw, mu, vec, w2cat, a2cat, g2, ones, trif, trib, band)


def _sm(x, m_a):
    return jnp.concatenate([jnp.where(m_a, x, 0), jnp.where(m_a, 0, x)], axis=0).astype(BF)


def _chunk_step(chains, blk, eye, m_a, bd):
    c = CHUNK
    sm = lambda x: _sm(x, m_a)
    pm = lambda x, y: jnp.dot(x.astype(BF), sm(y), preferred_element_type=F32)
    each = lambda fn, *cols: [fn(*args) for args in zip(*cols)]

    strict = [ch["strict"] for ch in chains]
    incl = [ch["incl"] for ch in chains]
    p2, q2, k2, r2, v2, s2 = ([ch[n] for ch in chains] for n in ("p", "q", "k", "r", "v", "s"))
    pr = each(lambda p, r: jnp.concatenate([p, r], axis=0), p2, r2)
    gps = each(lambda a, q, k, s: _dot_nt(a, jnp.concatenate([sm(q), sm(k), s.astype(BF)], axis=0)), pr, q2, k2, s2)
    gram = each(lambda g: g[:, :4 * c], gps)
    prs = each(lambda g: g[:, 4 * c:], gps)
    lm = each(lambda g, m: jnp.where(m, g[:c, :2 * c], 0.0), gram, strict)
    rhs = each(lambda g, m, v, ps: ps[:c] + pm(jnp.where(m, g[:c, 2 * c:], 0.0), v), gram, strict, v2, prs)
    dm = each(lambda l: jnp.where(blk, l, 0.0), lm)
    em = each(lambda l, d: l - d, lm, dm)
    d2 = each(pm, dm, dm)
    nd = each(lambda d, dd: pm(jnp.concatenate([eye - d, dd], axis=0), dd), dm, d2)
    n2 = each(lambda d, t: (eye - d) + t[:c], dm, nd)
    d4 = each(lambda t: t[c:], nd)
    nd = each(lambda n, dd: pm(jnp.concatenate([n, dd], axis=0), dd), n2, d4)
    n3 = each(lambda n, t: n + t[:c], n2, nd)
    d8 = each(lambda t: t[c:], nd)
    td = each(lambda n, dd: n + pm(n, dd), n3, d8)
    fx = each(lambda t, e, r: jnp.dot(t.astype(BF), jnp.concatenate([sm(e), sm(r)], axis=1), preferred_element_type=F32),
              td, em, rhs)
    f = each(lambda t: t[:, :2 * c], fx)
    x1 = each(lambda t: t[:, 2 * c:], fx)
    f2 = each(pm, f, f)
    zz = each(lambda x, ff: x + pm(ff, x), x1, f2)
    u = each(lambda ff, z: pm(ff, z) - z, f, zz)
    y = each(lambda g, m, ps, uu, v: ps[c:] + _dot(
        jnp.concatenate([jnp.where(m, g[c:, :2 * c], 0.0), jnp.where(m, g[c:, 2 * c:], 0.0)], axis=1),
        jnp.concatenate([sm(uu), sm(v)], axis=0)), gram, incl, prs, u, v2)
    upd = each(lambda uu, v, q, k: _dot_tn(jnp.concatenate([uu.astype(BF), v], axis=0), jnp.concatenate([q, k], axis=0)),
               u, v2, q2, k2)
    s_new = each(lambda s, up, ch: jnp.where(bd, s + up, 0.0) * ch["w"], s2, upd, chains)
    return y, s_new


def _rwkv_scan_kernel(pf_ref, qf_ref, kf_ref, rf_ref, vf_ref, wcf_ref, pb_ref, qb_ref, kb_ref, rb_ref, vb_ref, wcb_ref,
                      yf_ref, yb_ref, sf_ref, sb_ref, *, bwd_chunk):
    n = pl.program_id(1)
    c = CHUNK
    npairs = sf_ref.shape[0]

    @pl.when(n == 0)
    def _():
        sf_ref[...] = jnp.zeros_like(sf_ref)
        sb_ref[...] = jnp.zeros_like(sb_ref)

    t_i = lax.broadcasted_iota(jnp.int32, (c, 2 * c), 0)
    s_i = jnp.bitwise_and(lax.broadcasted_iota(jnp.int32, (c, 2 * c), 1), c - 1)
    m_a = lax.broadcasted_iota(jnp.int32, (1, LANES), 1) < RW_HEAD
    blk = (t_i // 16) == (s_i // 16)
    eye = (t_i == s_i).astype(F32)
    bd = (lax.broadcasted_iota(jnp.int32, (LANES, LANES), 0) // RW_HEAD) == \
         (lax.broadcasted_iota(jnp.int32, (LANES, LANES), 1) // RW_HEAD)
    per_blk = TOKEN_BLOCK // c
    wrow_f = wcf_ref[0, 0, pl.ds(n % per_blk, 1), :]
    wrow_b = wcb_ref[0, 0, pl.ds(per_blk + bwd_chunk(n) % per_blk, 1), :]

    dirs = (
        (pf_ref, qf_ref, kf_ref, rf_ref, vf_ref, wrow_f, yf_ref, sf_ref, s_i < t_i, s_i <= t_i),
        (pb_ref, qb_ref, kb_ref, rb_ref, vb_ref, wrow_b, yb_ref, sb_ref, s_i > t_i, s_i >= t_i),
    )
    chains, sinks = [], []
    for p_ref, q_ref, k_ref, r_ref, v_ref, wrow, y_ref, s_ref, strict, incl in dirs:
        for p in range(npairs):
            sl = slice(p * LANES, (p + 1) * LANES)
            chains.append(dict(p=p_ref[0, :, sl], q=q_ref[0, :, sl], k=k_ref[0, :, sl], r=r_ref[0, :, sl],
                               v=v_ref[0, :, sl], s=s_ref[p], w=wrow[:, sl], strict=strict, incl=incl))
            sinks.append((y_ref, s_ref, p, sl))
    ys, s_news = _chunk_step(chains, blk, eye, m_a, bd)
    for (y_ref, s_ref, p, sl), y, s_new in zip(sinks, ys, s_news):
        y_ref[0, :, sl] = y.astype(y_ref.dtype)
        s_ref[p] = s_new


def _rwkv_scan(v, pf, qf, kf, rf, pb, qb, kb, rb, wc, *, ctx_len):
    b, l, d = v.shape
    c = CHUNK
    nch = l // c
    nctx = ctx_len // c
    per_blk = TOKEN_BLOCK // c

    def bwd_chunk(n):
        return jnp.where(n < nctx, nctx - 1 - n, nch + nctx - 1 - n)

    fwd = pl.BlockSpec((1, c, d), lambda bi, n: (bi, n, 0))
    bwd = pl.BlockSpec((1, c, d), lambda bi, n: (bi, bwd_chunk(n), 0))
    wcf = pl.BlockSpec((1, 1, SUBLANES, d), lambda bi, n: (bi, n // per_blk, 0, 0))
    wcb = pl.BlockSpec((1, 1, SUBLANES, d), lambda bi, n: (bi, bwd_chunk(n) // per_blk, 0, 0))
    npairs = d // LANES
    kern = functools.partial(_rwkv_scan_kernel, bwd_chunk=bwd_chunk)
    return pl.pallas_call(
        kern,
        out_shape=[jax.ShapeDtypeStruct((b, l, d), BF)] * 2,
        grid=(b, nch),
        in_specs=[fwd] * 5 + [wcf] + [bwd] * 5 + [wcb],
        out_specs=[fwd, bwd],
        scratch_shapes=[pltpu.VMEM((npairs, LANES, LANES), F32)] * 2,
        compiler_params=_cparams(("parallel", "arbitrary")),
        name="rwkv_scan",
    )(pf, qf, kf, rf, v, wc, pb, qb, kb, rb, v, wc)


def _rwkv_readout_kernel(yf_ref, yb_ref, bon_ref, g_ref, vec_ref, ones_ref, o_ref):
    ones = ones_ref[...]
    y = yf_ref[...].astype(F32) + yb_ref[...].astype(F32)
    inv_n = 1.0 / RW_HEAD
    yc = y - _head_sums(y, ones, True) * inv_n
    var = _head_sums(yc * yc, ones, True) * inv_n
    yn = yc * lax.rsqrt(var + RW_GN_EPS) * vec_ref[0:1, :] + vec_ref[1:2, :]
    o_ref[...] = ((yn + bon_ref[...].astype(F32)) * g_ref[...].astype(F32)).astype(BF)


def _rwkv_readout(yf, yb, bon, g, vec, ones):
    m, d = yf.shape
    tm = TOKEN_BLOCK
    tok = pl.BlockSpec((tm, d), lambda i: (i, 0))
    return pl.pallas_call(
        _rwkv_readout_kernel,
        out_shape=jax.ShapeDtypeStruct((m, d), BF),
        grid=(m // tm,),
        in_specs=[tok, tok, tok, tok, pl.BlockSpec((SUBLANES, d), lambda i: (0, 0)),
                  pl.BlockSpec((ONES_TILE, ONES_TILE), lambda i: (0, 0))],
        out_specs=tok,
        compiler_params=_cparams(("parallel",)),
        name="rwkv_readout",
    )(yf, yb, bon, g, vec, ones)


CONV_ROWS = 32
CONV_LANES = 512


def _conv_kernel(z_ref, zp_ref, zn_ref, w_ref, vec_ref, o_ref, gp_ref, gs_ref, u_ref, *, nblocks):
    i = pl.program_id(1)
    tb = z_ref.shape[1]
    d = D_MODEL
    h = HALO

    def glu(zz):
        zz = zz.astype(F32)
        return zz[:, :d] * _sigmoid(zz[:, d:])

    prev_ok = i >= 2
    next_ok = jnp.logical_and(i >= 1, i <= nblocks - 2)
    gp_ref[0:h, :] = jnp.where(prev_ok, glu(zp_ref[0]), 0.0)
    gp_ref[h:h + tb, :] = glu(z_ref[0])
    gp_ref[h + tb:2 * h + tb, :] = jnp.where(next_ok, glu(zn_ref[0]), 0.0)
    rows = gs_ref.shape[1]
    for s in range(SUBLANES):
        gs_ref[s] = gp_ref[s:s + rows, :]

    off = h - CONV_WIDTH // 2
    grp = CONV_ROWS // SUBLANES
    for r0 in range(0, tb, CONV_ROWS):
        for c0 in range(0, d, CONV_LANES):
            acc = jnp.zeros((grp, SUBLANES, CONV_LANES), F32)
            for k in range(CONV_WIDTH):
                a, s = divmod(k + off, SUBLANES)
                win = gs_ref[s, r0 + a * SUBLANES:r0 + a * SUBLANES + CONV_ROWS, c0:c0 + CONV_LANES]
                acc = acc + win.reshape(grp, SUBLANES, CONV_LANES) * w_ref[k, :, c0:c0 + CONV_LANES][None]
            u_ref[r0:r0 + CONV_ROWS, c0:c0 + CONV_LANES] = acc.reshape(CONV_ROWS, CONV_LANES)

    u = u_ref[...] + vec_ref[0:1, :]
    mu = jnp.mean(u, axis=-1, keepdims=True)
    uc = u - mu
    var = jnp.mean(uc * uc, axis=-1, keepdims=True)
    y = uc * lax.rsqrt(var + EPS) * vec_ref[1:2, :] + vec_ref[2:3, :]
    o_ref[0] = (y * _sigmoid(y)).astype(BF)


def _conv_branch(z_cv, w, vec):
    b, l, cols = z_cv.shape
    tb = TOKEN_BLOCK
    nb = l // tb
    d = D_MODEL
    hb = tb // HALO
    nhalo = l // HALO
    shifted_rows = tb + 2 * HALO - SUBLANES
    kern = functools.partial(_conv_kernel, nblocks=nb)
    return pl.pallas_call(
        kern,
        out_shape=jax.ShapeDtypeStruct((b, l, d), BF),
        grid=(b, nb),
        in_specs=[
            pl.BlockSpec((1, tb, cols), lambda bi, i: (bi, i, 0)),
            pl.BlockSpec((1, HALO, cols), lambda bi, i: (bi, jnp.maximum(i * hb - 1, 0), 0)),
            pl.BlockSpec((1, HALO, cols), lambda bi, i: (bi, jnp.minimum((i + 1) * hb, nhalo - 1), 0)),
            pl.BlockSpec((CONV_WIDTH, SUBLANES, d), lambda bi, i: (0, 0, 0)),
            pl.BlockSpec((SUBLANES, d), lambda bi, i: (0, 0)),
        ],
        out_specs=pl.BlockSpec((1, tb, d), lambda bi, i: (bi, i, 0)),
        scratch_shapes=[pltpu.VMEM((tb + 2 * HALO, d), F32), pltpu.VMEM((SUBLANES, shifted_rows, d), F32),
                        pltpu.VMEM((tb, d), F32)],
        compiler_params=_cparams(("parallel", "parallel")),
        name="conv_branch",
    )(z_cv, z_cv, z_cv, w, vec)


def _att_prep_kernel(z_ref, cos_ref, sin_ref, gq_ref, gk_ref, q_ref, k_ref, v_ref):
    cos = cos_ref[...]
    sin = sin_ref[...]
    hd = ATT_HEAD

    def norm_rope(x, g):
        xn = _rms(x.astype(F32), g)
        return xn * cos + pltpu.roll(xn, hd // 2, 1) * sin

    q_scale = (hd ** -0.5) * LOG2_E
    for h in range(ATT_Q_HEADS):
        sl = slice(h * hd, (h + 1) * hd)
        q_ref[0, :, sl] = (norm_rope(z_ref[0, :, sl], gq_ref[...]) * q_scale).astype(BF)
    for h in range(ATT_KV_HEADS):
        sl = slice(h * hd, (h + 1) * hd)
        k_ref[0, :, sl] = norm_rope(z_ref[0, :, D_MODEL + h * hd:D_MODEL + (h + 1) * hd], gk_ref[...]).astype(BF)
        v_ref[0, :, 2 * h * hd:(2 * h + 1) * hd] = z_ref[0, :, D_MODEL + KV_COLS + h * hd:D_MODEL + KV_COLS + (h + 1) * hd]
        v_ref[0, :, (2 * h + 1) * hd:(2 * h + 2) * hd] = jnp.ones((z_ref.shape[1], hd), BF)


def _att_prep(z_at, cos_t, sin_t, gq, gk):
    b, l, cols = z_at.shape
    tb = TOKEN_BLOCK
    tok = lambda bi, i: (bi, i, 0)
    return pl.pallas_call(
        _att_prep_kernel,
        out_shape=[jax.ShapeDtypeStruct((b, l, D_MODEL), BF), jax.ShapeDtypeStruct((b, l, KV_COLS), BF),
                   jax.ShapeDtypeStruct((b, l, 2 * KV_COLS), BF)],
        grid=(b, l // tb),
        in_specs=[
            pl.BlockSpec((1, tb, cols), tok),
            pl.BlockSpec((tb, ATT_HEAD), lambda bi, i: (i, 0)),
            pl.BlockSpec((tb, ATT_HEAD), lambda bi, i: (i, 0)),
            pl.BlockSpec((1, ATT_HEAD), lambda bi, i: (0, 0)),
            pl.BlockSpec((1, ATT_HEAD), lambda bi, i: (0, 0)),
        ],
        out_specs=[pl.BlockSpec((1, tb, D_MODEL), tok), pl.BlockSpec((1, tb, KV_COLS), tok),
                   pl.BlockSpec((1, tb, 2 * KV_COLS), tok)],
        compiler_params=_cparams(("parallel", "parallel")),
        name="att_prep",
    )(z_at, cos_t, sin_t, gq, gk)


def _attn_kernel(q_ref, k_ref, v_ref, o_ref, *, ctx_len):
    hd = ATT_HEAD

    def run(nk):
        s_prev = p_prev = None
        tq = q_ref.shape[1]
        nunits = ATT_Q_HEADS // ATT_STACK
        for un in range(nunits + 2):
            s_new = None
            if un < nunits:
                hk = un * ATT_STACK // ATT_GROUP
                qs = [q_ref[0, :, h * hd:(h + 1) * hd] for h in range(un * ATT_STACK, (un + 1) * ATT_STACK)]
                s_new = _dot_nt(qs[0] if ATT_STACK == 1 else jnp.concatenate(qs, axis=0), k_ref[0, 0:nk, hk * hd:(hk + 1) * hd])
            p_new = None
            if s_prev is not None:
                p_new = jnp.exp2(s_prev - jnp.max(s_prev, axis=-1, keepdims=True)).astype(BF)
            if p_prev is not None:
                uo = un - 2
                hk = uo * ATT_STACK // ATT_GROUP
                o = jnp.dot(p_prev, v_ref[0, 0:nk, 2 * hk * hd:(2 * hk + 2) * hd], preferred_element_type=F32)
                o = (o[:, :hd] / o[:, hd:]).astype(BF)
                for g in range(ATT_STACK):
                    ho = uo * ATT_STACK + g
                    o_ref[0, :, ho * hd:(ho + 1) * hd] = o[g * tq:(g + 1) * tq]
            s_prev, p_prev = s_new, p_new

    @pl.when(pl.program_id(1) == 0)
    def _():
        run(ctx_len)

    @pl.when(pl.program_id(1) > 0)
    def _():
        run(k_ref.shape[1])


def _attention(q, k, v, *, ctx_len):
    b, l, d = q.shape
    tb = TOKEN_BLOCK
    kern = functools.partial(_attn_kernel, ctx_len=ctx_len)
    seq = lambda bi, i: (bi, 0, 0)
    return pl.pallas_call(
        kern,
        out_shape=jax.ShapeDtypeStruct((b, l, d), BF),
        grid=(b, l // tb),
        in_specs=[pl.BlockSpec((1, tb, d), lambda bi, i: (bi, i, 0)), pl.BlockSpec((1, l, KV_COLS), seq),
                  pl.BlockSpec((1, l, 2 * KV_COLS), seq)],
        out_specs=pl.BlockSpec((1, tb, d), lambda bi, i: (bi, i, 0)),
        compiler_params=_cparams(("parallel", "parallel")),
        name="attention",
    )(q, k, v)


def _merge_kernel(orw_ref, ocv_ref, oat_ref, zg_ref, x_ref, wb_ref, wo_ref, g_ref, ml_ref, mc_ref, o_ref,
                  *, blocks_per_seq, ctx_len):
    d = D_MODEL
    is_ctx = _ctx_rows(x_ref.shape[0], blocks_per_seq, ctx_len)
    m = None
    for n, o_n in enumerate((orw_ref, ocv_ref, oat_ref)):
        t = _sigmoid(zg_ref[:, n * d:(n + 1) * d].astype(F32)) * jnp.dot(o_n[...], wb_ref[n], preferred_element_type=F32)
        m = t if m is None else m + t
    out = _dot(m, wo_ref[...])
    o_ref[...] = x_ref[...] + _mod_rows(ml_ref, mc_ref, 2, is_ctx) * _rms(out, g_ref[...])


def _merge(orw, ocv, oat, zg, x2, wb, wo, gain, modl, modc, *, seq_len, ctx_len):
    m, d = x2.shape
    tm = _row_block(seq_len)
    bps = seq_len // tm
    tok = pl.BlockSpec((tm, d), lambda i: (i, 0))
    kern = functools.partial(_merge_kernel, blocks_per_seq=bps, ctx_len=ctx_len)
    return pl.pallas_call(
        kern,
        out_shape=jax.ShapeDtypeStruct((m, d), F32),
        grid=(m // tm,),
        in_specs=[
            tok, tok, tok,
            pl.BlockSpec((tm, 3 * d), lambda i: (i, 0)),
            tok,
            _resident(wb.shape),
            _resident(wo.shape),
            pl.BlockSpec((1, d), lambda i: (0, 0)),
            pl.BlockSpec((1, SUBLANES, d), lambda i: (i // bps, 0, 0)),
            pl.BlockSpec((SUBLANES, d), lambda i: (0, 0)),
        ],
        out_specs=tok,
        compiler_params=_cparams(("parallel",)),
        name="merge",
    )(orw, ocv, oat, zg, x2, wb, wo, gain, modl, modc)


def _mlp_kernel(x_ref, w1_ref, w2_ref, gpre_ref, gpost_ref, ml_ref, mc_ref, o_ref, *, blocks_per_seq, ctx_len):
    is_ctx = _ctx_rows(x_ref.shape[0], blocks_per_seq, ctx_len)
    x = x_ref[...]
    h = _rms(x, gpre_ref[...]) * (1.0 + _mod_rows(ml_ref, mc_ref, 4, is_ctx)) + _mod_rows(ml_ref, mc_ref, 3, is_ctx)
    hb = h.astype(BF)
    out = None
    for c0 in range(0, w1_ref.shape[1], FF_TILE):
        a = jnp.maximum(jnp.dot(hb, w1_ref[:, c0:c0 + FF_TILE], preferred_element_type=F32), 0.0)
        t = _dot(a * a, w2_ref[c0:c0 + FF_TILE, :])
        out = t if out is None else out + t
    o_ref[...] = x + _mod_rows(ml_ref, mc_ref, 5, is_ctx) * _rms(out, gpost_ref[...])


def _mlp(x2, w1, w2, gpre, gpost, modl, modc, *, seq_len, ctx_len):
    m, d = x2.shape
    tm = _row_block(seq_len)
    bps = seq_len // tm
    tok = pl.BlockSpec((tm, d), lambda i: (i, 0))
    kern = functools.partial(_mlp_kernel, blocks_per_seq=bps, ctx_len=ctx_len)
    return pl.pallas_call(
        kern,
        out_shape=jax.ShapeDtypeStruct((m, d), F32),
        grid=(m // tm,),
        in_specs=[
            tok,
            _resident(w1.shape),
            _resident(w2.shape),
            pl.BlockSpec((1, d), lambda i: (0, 0)),
            pl.BlockSpec((1, d), lambda i: (0, 0)),
            pl.BlockSpec((1, SUBLANES, d), lambda i: (i // bps, 0, 0)),
            pl.BlockSpec((SUBLANES, d), lambda i: (0, 0)),
        ],
        out_specs=tok,
        compiler_params=_cparams(("parallel",)),
        name="mlp",
    )(x2, w1, w2, gpre, gpost, modl, modc)


def _pad_rows(a, rows):
    return jnp.pad(a, ((0, rows - a.shape[0]), (0, 0)))


def _rope_tables(seq, ctx_len):
    rows = seq // GRID_W
    row = jnp.repeat(jnp.arange(rows), GRID_W).astype(F32)
    col = jnp.tile(jnp.arange(GRID_W), rows).astype(F32)
    axis_dim = ATT_HEAD // 2
    freqs = ROPE_THETA ** (-jnp.arange(0, axis_dim, 2, dtype=F32) / axis_dim)
    ang = jnp.concatenate([row[:, None] * freqs, col[:, None] * freqs], axis=-1)
    cos, sin = jnp.cos(ang), jnp.sin(ang)
    cos_t = jnp.concatenate([cos, cos], axis=-1)
    sin_t = jnp.concatenate([-sin, sin], axis=-1)
    cos_t = jnp.concatenate([jnp.ones((ctx_len, ATT_HEAD), F32), cos_t], axis=0)
    sin_t = jnp.concatenate([jnp.zeros((ctx_len, ATT_HEAD), F32), sin_t], axis=0)
    return cos_t, sin_t


def _chunk_tri(upper):
    t = jnp.arange(TOKEN_BLOCK)
    same = (t[:, None] // CHUNK) == (t[None, :] // CHUNK)
    tri = (t[None, :] >= t[:, None]) if upper else (t[None, :] <= t[:, None])
    return jnp.logical_and(same, tri).astype(BF)


def kernel(x, c, ctx, c_ctx, w_mod, b_mod, norm_mix_pre, norm_mix_post, norm_mlp_pre, norm_mlp_post, w_in, rw_mu, rw_w0, rw_w2, rw_a0, rw_a2, rw_g2, rw_k_k, rw_k_a, rw_r_k, rw_ln_g, rw_ln_b, cv_dw_w, cv_dw_b, cv_ln_g, cv_ln_b, at_q_norm, at_k_norm, w_branch, w_out, w_ff1, w_ff2):
    b, s, d = x.shape
    ctx_len = ctx.shape[1]
    depth = w_in.shape[0]
    l = ctx_len + s
    assert d == D_MODEL and ctx_len == TOKEN_BLOCK and s % TOKEN_BLOCK == 0

    mod_rows = -(-(b + 1) // SUBLANES) * SUBLANES
    cvec = _pad_rows(jnp.concatenate([c, c_ctx[None, :]], axis=0), mod_rows)
    mods = _modulation(cvec, w_mod, b_mod).reshape(depth, mod_rows, N_MOD, d)
    mods = jnp.pad(mods, ((0, 0), (0, 0), (0, SUBLANES - N_MOD), (0, 0)))

    cos_t, sin_t = _rope_tables(s, ctx_len)
    half_perm = jnp.concatenate([jnp.arange(0, ATT_HEAD, 2), jnp.arange(1, ATT_HEAD, 2)])
    qk_perm = (jnp.arange(ATT_Q_HEADS + ATT_KV_HEADS)[:, None] * ATT_HEAD + half_perm[None, :]).reshape(-1)
    ones_blk = (jnp.arange(ONES_TILE)[:, None] // RW_HEAD == jnp.arange(ONES_TILE)[None, :] // RW_HEAD).astype(BF)
    trif, trib = _chunk_tri(False), _chunk_tri(True)
    tpos = jnp.arange(TOKEN_BLOCK)
    shift_band = (0.5 * (jnp.abs(tpos[:, None] - tpos[None, :]) == 1) - (tpos[:, None] == tpos[None, :])).astype(BF)
    zpad64 = jnp.zeros((64, d), F32)

    xu = jnp.concatenate([ctx, x], axis=1).reshape(b * l, d)

    for li in range(depth):
        modl, modc = mods[li, :b], mods[li, b]
        w_l = w_in[li]
        c0, c1, c2 = RWKV_COLS, RWKV_COLS + 2 * d, RWKV_COLS + 2 * d + ATT_COLS
        w_rw = w_l[:, :c0].astype(BF)
        w_cv = w_l[:, c0:c1].astype(BF)
        w_at = w_l[:, c1:c2]
        w_at = jnp.concatenate([w_at[:, qk_perm], w_at[:, d + KV_COLS:]], axis=1).astype(BF)
        w_gt = w_l[:, c2:].astype(BF)
        z_rw, z_cv, z_at, z_gt = _modproj(xu, norm_mix_pre[li][None, :], modl, modc, (w_rw, w_cv, w_at, w_gt),
                                          (1152, 1024, 768, 1024), seq_len=l, ctx_len=ctx_len, shift_idx=0, scale_idx=1)
        z_rw = z_rw.reshape(b, l, RWKV_COLS)
        z_cv = z_cv.reshape(b, l, 2 * d)
        z_at = z_at.reshape(b, l, ATT_COLS)

        rw_vec = _pad_rows(jnp.stack([rw_w0[li, 0], rw_w0[li, 1], rw_a0[li, 0], rw_a0[li, 1], rw_k_k[li], rw_k_a[li],
                                      rw_r_k[li].reshape(-1)]), SUBLANES)
        w2cat = jnp.concatenate([jnp.concatenate([rw_w2[li, 0], zpad64], axis=0),
                                 jnp.concatenate([zpad64, rw_w2[li, 1]], axis=0)], axis=1).astype(BF)
        a2cat = jnp.concatenate([jnp.concatenate([rw_a2[li, 0], zpad64], axis=0),
                                 jnp.concatenate([zpad64, rw_a2[li, 1]], axis=0)], axis=1).astype(BF)
        v, pf, qf, kf, rf, pb, qb, kb, rb, wc, g, bon = _rwkv_prep(
            z_rw, rw_mu[li][None, :], rw_vec, w2cat, a2cat, rw_g2[li].astype(BF), ones_blk, trif, trib, shift_band)
        yf, yb = _rwkv_scan(v, pf, qf, kf, rf, pb, qb, kb, rb, wc, ctx_len=ctx_len)
        ln_vec = _pad_rows(jnp.stack([rw_ln_g[li], rw_ln_b[li]]), SUBLANES)
        o_rw = _rwkv_readout(yf.reshape(b * l, d), yb.reshape(b * l, d), bon.reshape(b * l, d), g.reshape(b * l, d),
                             ln_vec, ones_blk)

        cv_vec = _pad_rows(jnp.stack([cv_dw_b[li], cv_ln_g[li], cv_ln_b[li]]), SUBLANES)
        dw_w = jnp.broadcast_to(cv_dw_w[li][:, None, :], (CONV_WIDTH, SUBLANES, d))
        o_cv = _conv_branch(z_cv, dw_w, cv_vec).reshape(b * l, d)

        q, k, vv = _att_prep(z_at, cos_t, sin_t, at_q_norm[li][half_perm][None, :], at_k_norm[li][half_perm][None, :])
        o_at = _attention(q, k, vv, ctx_len=ctx_len).reshape(b * l, d)

        xu = _merge(o_rw, o_cv, o_at, z_gt, xu, w_branch[li].astype(BF), w_out[li].astype(BF),
                    norm_mix_post[li][None, :], modl, modc, seq_len=l, ctx_len=ctx_len)
        xu = _mlp(xu, w_ff1[li].astype(BF), w_ff2[li].astype(BF), norm_mlp_pre[li][None, :], norm_mlp_post[li][None, :],
                  modl, modc, seq_len=l, ctx_len=ctx_len)

    return xu.reshape(b, l, d)[:, ctx_len:, :]
```

```python
import functools
import math

import jax
import jax.numpy as jnp
from jax import lax
from jax.experimental import pallas as pl
from jax.experimental.pallas import tpu as pltpu

F32 = jnp.float32
BF = jnp.bfloat16

D_MODEL = 1024
N_MOD = 6
EPS = 1e-6
RW_HEAD = 64
RW_HEADS = 16
RW_GN_EPS = 64e-5
RW_LORA_COLS = 384
RWKV_COLS = 3 * D_MODEL + RW_LORA_COLS
CONV_WIDTH = 31
HALO = 16
ONES_TILE = 256
ATT_HEAD = 128
ATT_Q_HEADS = 8
ATT_KV_HEADS = 2
ATT_GROUP = ATT_Q_HEADS // ATT_KV_HEADS
ATT_STACK = 2
KV_COLS = ATT_KV_HEADS * ATT_HEAD
ATT_COLS = D_MODEL + 2 * KV_COLS
GRID_W = 64
ROPE_THETA = 10000.0
D_FF = 4 * D_MODEL

LANES = 128
SUBLANES = 8
CHUNK = 64
TOKEN_BLOCK = 256
PROJ_ROWS = 768
FF_TILE = 1024
VMEM_LIMIT = 56 * 1024 * 1024
EXP_NEG_HALF = math.exp(-0.5)
LOG2_E = math.log2(math.e)


def _cparams(sem):
    return pltpu.CompilerParams(dimension_semantics=sem, vmem_limit_bytes=VMEM_LIMIT)


def _dot(a, b):
    return jnp.dot(a.astype(BF), b.astype(BF), preferred_element_type=F32)


def _dot_nt(a, b):
    return lax.dot_general(a.astype(BF), b.astype(BF), (((1,), (1,)), ((), ())), preferred_element_type=F32)


def _dot_tn(a, b):
    return lax.dot_general(a.astype(BF), b.astype(BF), (((0,), (0,)), ((), ())), preferred_element_type=F32)


def _split_dot(x, w):
    hi = x.astype(BF)
    lo = (x - hi.astype(F32)).astype(BF)
    return jnp.dot(hi, w, preferred_element_type=F32) + jnp.dot(lo, w, preferred_element_type=F32)


def _head_sums(x, ones, split):
    w = ones.shape[0]
    dot = _split_dot if split else _dot
    return jnp.concatenate([dot(x[:, n * w:(n + 1) * w], ones) for n in range(x.shape[1] // w)], axis=1)


def _sigmoid(x):
    return 1.0 / (1.0 + jnp.exp(-x))


def _rms(x, g):
    return x * lax.rsqrt(jnp.mean(x * x, axis=-1, keepdims=True) + EPS) * g


def _mod_kernel(c_ref, w_ref, b_ref, o_ref):
    c = c_ref[...]
    o_ref[0] = _dot(c * _sigmoid(c), w_ref[0]) + b_ref[0]


def _modulation(cvec, w_mod, b_mod):
    depth, d, n = w_mod.shape
    rows = cvec.shape[0]
    tn = 1536
    return pl.pallas_call(
        _mod_kernel,
        out_shape=jax.ShapeDtypeStruct((depth, rows, n), F32),
        grid=(depth, n // tn),
        in_specs=[
            pl.BlockSpec((rows, d), lambda l, j: (0, 0)),
            pl.BlockSpec((1, d, tn), lambda l, j: (l, 0, j)),
            pl.BlockSpec((1, 1, tn), lambda l, j: (l, 0, j)),
        ],
        out_specs=pl.BlockSpec((1, rows, tn), lambda l, j: (l, 0, j)),
        compiler_params=_cparams(("parallel", "parallel")),
        name="modulation",
    )(cvec, w_mod, b_mod.reshape(depth, 1, n))


def _mod_rows(ml_ref, mc_ref, idx, is_ctx):
    return jnp.where(is_ctx, mc_ref[idx:idx + 1, :], ml_ref[0, idx:idx + 1, :])


def _ctx_rows(tm, blocks_per_seq, ctx_len):
    row = (pl.program_id(0) % blocks_per_seq) * tm + lax.broadcasted_iota(jnp.int32, (tm, 1), 0)
    return row < ctx_len


def _row_block(seq_len):
    return max(t for t in range(LANES, PROJ_ROWS + 1, LANES) if seq_len % t == 0)


def _resident(shape):
    return pl.BlockSpec(shape, lambda *_: (0,) * len(shape), pipeline_mode=pl.Buffered(1))


def _modproj_kernel(x_ref, g_ref, ml_ref, mc_ref, *rest, starts, shift_idx, scale_idx, blocks_per_seq, ctx_len):
    ngroups = len(starts) - 1
    w_refs, o_refs, h_ref = rest[:ngroups], rest[ngroups:2 * ngroups], rest[2 * ngroups]
    j = pl.program_id(1)

    @pl.when(j == 0)
    def _():
        xn = _rms(x_ref[...], g_ref[...])
        is_ctx = _ctx_rows(x_ref.shape[0], blocks_per_seq, ctx_len)
        scale = _mod_rows(ml_ref, mc_ref, scale_idx, is_ctx)
        shift = _mod_rows(ml_ref, mc_ref, shift_idx, is_ctx)
        h_ref[...] = (xn * (1.0 + scale) + shift).astype(BF)

    for k in range(ngroups):
        tn = o_refs[k].shape[1]
        for t in range(starts[k + 1] - starts[k]):
            @pl.when(j == starts[k] + t)
            def _(k=k, t=t, tn=tn):
                o_refs[k][...] = jnp.dot(h_ref[...], w_refs[k][:, t * tn:(t + 1) * tn],
                                         preferred_element_type=F32).astype(o_refs[k].dtype)


def _modproj(x2, gain, modl, modc, weights, tiles, *, seq_len, ctx_len, shift_idx, scale_idx):
    m, d = x2.shape
    tm = _row_block(seq_len)
    bps = seq_len // tm
    nblk = [w.shape[1] // tn for w, tn in zip(weights, tiles)]
    starts = [0]
    for nb in nblk:
        starts.append(starts[-1] + nb)

    def col(k):
        return lambda j: jnp.clip(j - starts[k], 0, nblk[k] - 1)

    kern = functools.partial(_modproj_kernel, starts=tuple(starts), shift_idx=shift_idx, scale_idx=scale_idx,
                             blocks_per_seq=bps, ctx_len=ctx_len)
    w_specs = [_resident(w.shape) for w in weights]
    o_specs = [pl.BlockSpec((tm, tn), lambda i, j, c=col(k): (i, c(j))) for k, tn in enumerate(tiles)]
    return pl.pallas_call(
        kern,
        out_shape=[jax.ShapeDtypeStruct((m, w.shape[1]), BF) for w in weights],
        grid=(m // tm, starts[-1]),
        in_specs=[
            pl.BlockSpec((tm, d), lambda i, j: (i, 0)),
            pl.BlockSpec((1, d), lambda i, j: (0, 0)),
            pl.BlockSpec((1, SUBLANES, d), lambda i, j: (i // bps, 0, 0)),
            pl.BlockSpec((SUBLANES, d), lambda i, j: (0, 0)),
        ] + w_specs,
        out_specs=o_specs,
        scratch_shapes=[pltpu.VMEM((tm, d), BF)],
        compiler_params=_cparams(("parallel", "arbitrary")),
        name="modproj",
    )(x2, gain, modl, modc, *weights)


def _rwkv_prep_kernel(z_ref, zp_ref, zn_ref, mu_ref, vec_ref, w2_ref, a2_ref, g2_ref, ones_ref, trif_ref, trib_ref, band_ref,
                      v_ref, pf_ref, qf_ref, kf_ref, rf_ref, pb_ref, qb_ref, kb_ref, rb_ref, wc_ref, g_ref, bon_ref,
                      *, nblocks):
    i = pl.program_id(1)
    zb = z_ref[0]
    z = zb.astype(F32)
    tb = z.shape[0]
    prev_ok = i >= 2
    next_ok = jnp.logical_and(i >= 1, i <= nblocks - 2)
    zp_row = jnp.where(prev_ok, zp_ref[0, HALO - 1:HALO, :].astype(F32), 0.0)
    zn_row = jnp.where(next_ok, zn_ref[0, 0:1, :].astype(F32), 0.0)
    delta = jnp.dot(band_ref[...], zb, preferred_element_type=F32)
    sub = lax.broadcasted_iota(jnp.int32, (SUBLANES, 1), 0)
    top = delta[0:SUBLANES] + jnp.where(sub == 0, 0.5 * zp_row, 0.0)
    bot = delta[tb - SUBLANES:tb] + jnp.where(sub == SUBLANES - 1, 0.5 * zn_row, 0.0)
    delta = jnp.concatenate([top, delta[SUBLANES:tb - SUBLANES], bot], axis=0)
    zs = z + mu_ref[...] * delta

    d = D_MODEL
    r = zs[:, 0:d]
    k = zs[:, d:2 * d]
    v = zs[:, 2 * d:3 * d]
    w_lo = _dot(jnp.tanh(zs[:, 3 * d:3 * d + 128]), w2_ref[...])
    a_lo = _dot(zs[:, 3 * d + 128:3 * d + 256], a2_ref[...])
    g_ref[0] = _dot(_sigmoid(zs[:, 3 * d + 256:3 * d + 384]), g2_ref[...]).astype(BF)

    vec = vec_ref[...]
    ones = ones_ref[...]
    kk0 = k * vec[4:5]
    kk = kk0 * jnp.minimum(lax.rsqrt(_head_sums(kk0 * kk0, ones, True)), 1e12)

    keysum = None
    outs = ((pf_ref, qf_ref, kf_ref, rf_ref, trif_ref), (pb_ref, qb_ref, kb_ref, rb_ref, trib_ref))
    for dr, (p_ref, q_ref, k_ref, r_ref, tri_ref) in enumerate(outs):
        w_raw = vec[dr:dr + 1] + w_lo[:, dr * d:(dr + 1) * d]
        lw = -EXP_NEG_HALF * _sigmoid(w_raw)
        a = _sigmoid(vec[2 + dr:3 + dr] + a_lo[:, dr * d:(dr + 1) * d])
        key = k * (1.0 + (a - 1.0) * vec[5:6])
        cum = _split_dot_left(tri_ref[...], lw)
        wt = jnp.exp(cum)
        iw = jnp.exp(-cum)
        p_ref[0] = (kk * jnp.exp(cum - lw)).astype(BF)
        q_ref[0] = (a * kk * iw).astype(BF)
        k_ref[0] = (key * iw).astype(BF)
        r_ref[0] = (r * wt).astype(BF)
        for c in range(tb // CHUNK):
            last = c * CHUNK + (CHUNK - 1 if dr == 0 else 0)
            wc_ref[0, 0, dr * 4 + c:dr * 4 + c + 1, :] = wt[last:last + 1, :]
        keysum = key if keysum is None else keysum + key

    bon_ref[0] = (_head_sums(r * keysum * vec[6:7], ones, False) * v).astype(BF)
    v_ref[0] = v.astype(BF)


def _split_dot_left(w, x):
    hi = x.astype(BF)
    lo = (x - hi.astype(F32)).astype(BF)
    return jnp.dot(w, hi, preferred_element_type=F32) + jnp.dot(w, lo, preferred_element_type=F32)


def _rwkv_prep(z_rw, mu, vec, w2cat, a2cat, g2, ones, trif, trib, band):
    b, l, cols = z_rw.shape
    tb = TOKEN_BLOCK
    nb = l // tb
    d = D_MODEL
    hb = tb // HALO
    nhalo = l // HALO
    tok = lambda bi, i: (bi, i, 0)
    const2 = lambda bi, i: (0, 0)
    seq_bf = jax.ShapeDtypeStruct((b, l, d), BF)
    out_tok = pl.BlockSpec((1, tb, d), tok)
    kern = functools.partial(_rwkv_prep_kernel, nblocks=nb)
    return pl.pallas_call(
        kern,
        out_shape=[seq_bf] * 9 + [jax.ShapeDtypeStruct((b, nb, SUBLANES, d), F32), seq_bf, seq_bf],
        grid=(b, nb),
        in_specs=[
            pl.BlockSpec((1, tb, cols), tok),
            pl.BlockSpec((1, HALO, cols), lambda bi, i: (bi, jnp.maximum(i * hb - 1, 0), 0)),
            pl.BlockSpec((1, HALO, cols), lambda bi, i: (bi, jnp.minimum((i + 1) * hb, nhalo - 1), 0)),
            pl.BlockSpec((1, cols), const2),
            pl.BlockSpec((SUBLANES, d), const2),
            pl.BlockSpec((LANES, 2 * d), const2),
            pl.BlockSpec((LANES, 2 * d), const2),
            pl.BlockSpec((LANES, d), const2),
            pl.BlockSpec((ONES_TILE, ONES_TILE), const2),
            pl.BlockSpec((tb, tb), const2),
            pl.BlockSpec((tb, tb), const2),
            pl.BlockSpec((tb, tb), const2),
        ],
        out_specs=[out_tok] * 9 + [pl.BlockSpec((1, 1, SUBLANES, d), lambda bi, i: (bi, i, 0, 0)), out_tok, out_tok],
        compiler_params=_cparams(("parallel", "parallel")),
        name="rwkv_prep",
    )(z_rw, z_rw, z_rw, mu, vec, w2cat, a2cat, g2, ones, trif, trib, band)


def _sm(x, m_a):
    return jnp.concatenate([jnp.where(m_a, x, 0), jnp.where(m_a, 0, x)], axis=0).astype(BF)


def _chunk_step(chains, blk, eye, m_a, bd):
    c = CHUNK
    sm = lambda x: _sm(x, m_a)
    pm = lambda x, y: jnp.dot(x.astype(BF), sm(y), preferred_element_type=F32)
    each = lambda fn, *cols: [fn(*args) for args in zip(*cols)]

    strict = [ch["strict"] for ch in chains]
    incl = [ch["incl"] for ch in chains]
    p2, q2, k2, r2, v2, s2 = ([ch[n] for ch in chains] for n in ("p", "q", "k", "r", "v", "s"))
    pr = each(lambda p, r: jnp.concatenate([p, r], axis=0), p2, r2)
    gps = each(lambda a, q, k, s: _dot_nt(a, jnp.concatenate([sm(q), sm(k), s.astype(BF)], axis=0)), pr, q2, k2, s2)
    gram = each(lambda g: g[:, :4 * c], gps)
    prs = each(lambda g: g[:, 4 * c:], gps)
    lm = each(lambda g, m: jnp.where(m, g[:c, :2 * c], 0.0), gram, strict)
    rhs = each(lambda g, m, v, ps: ps[:c] + pm(jnp.where(m, g[:c, 2 * c:], 0.0), v), gram, strict, v2, prs)
    dm = each(lambda l: jnp.where(blk, l, 0.0), lm)
    em = each(lambda l, d: l - d, lm, dm)
    d2 = each(pm, dm, dm)
    nd = each(lambda d, dd: pm(jnp.concatenate([eye - d, dd], axis=0), dd), dm, d2)
    n2 = each(lambda d, t: (eye - d) + t[:c], dm, nd)
    d4 = each(lambda t: t[c:], nd)
    nd = each(lambda n, dd: pm(jnp.concatenate([n, dd], axis=0), dd), n2, d4)
    n3 = each(lambda n, t: n + t[:c], n2, nd)
    d8 = each(lambda t: t[c:], nd)
    td = each(lambda n, dd: n + pm(n, dd), n3, d8)
    fx = each(lambda t, e, r: jnp.dot(t.astype(BF), jnp.concatenate([sm(e), sm(r)], axis=1), preferred_element_type=F32),
              td, em, rhs)
    f = each(lambda t: t[:, :2 * c], fx)
    x1 = each(lambda t: t[:, 2 * c:], fx)
    f2 = each(pm, f, f)
    zz = each(lambda x, ff: x + pm(ff, x), x1, f2)
    u = each(lambda ff, z: pm(ff, z) - z, f, zz)
    y = each(lambda g, m, ps, uu, v: ps[c:] + _dot(
        jnp.concatenate([jnp.where(m, g[c:, :2 * c], 0.0), jnp.where(m, g[c:, 2 * c:], 0.0)], axis=1),
        jnp.concatenate([sm(uu), sm(v)], axis=0)), gram, incl, prs, u, v2)
    upd = each(lambda uu, v, q, k: _dot_tn(jnp.concatenate([uu.astype(BF), v], axis=0), jnp.concatenate([q, k], axis=0)),
               u, v2, q2, k2)
    s_new = each(lambda s, up, ch: jnp.where(bd, s + up, 0.0) * ch["w"], s2, upd, chains)
    return y, s_new


def _rwkv_scan_kernel(pf_ref, qf_ref, kf_ref, rf_ref, vf_ref, wcf_ref, pb_ref, qb_ref, kb_ref, rb_ref, vb_ref, wcb_ref,
                      yf_ref, yb_ref, sf_ref, sb_ref, *, bwd_chunk):
    n = pl.program_id(1)
    c = CHUNK
    npairs = sf_ref.shape[0]

    @pl.when(n == 0)
    def _():
        sf_ref[...] = jnp.zeros_like(sf_ref)
        sb_ref[...] = jnp.zeros_like(sb_ref)

    t_i = lax.broadcasted_iota(jnp.int32, (c, 2 * c), 0)
    s_i = jnp.bitwise_and(lax.broadcasted_iota(jnp.int32, (c, 2 * c), 1), c - 1)
    m_a = lax.broadcasted_iota(jnp.int32, (1, LANES), 1) < RW_HEAD
    blk = (t_i // 16) == (s_i // 16)
    eye = (t_i == s_i).astype(F32)
    bd = (lax.broadcasted_iota(jnp.int32, (LANES, LANES), 0) // RW_HEAD) == \
         (lax.broadcasted_iota(jnp.int32, (LANES, LANES), 1) // RW_HEAD)
    per_blk = TOKEN_BLOCK // c
    wrow_f = wcf_ref[0, 0, pl.ds(n % per_blk, 1), :]
    wrow_b = wcb_ref[0, 0, pl.ds(per_blk + bwd_chunk(n) % per_blk, 1), :]

    dirs = (
        (pf_ref, qf_ref, kf_ref, rf_ref, vf_ref, wrow_f, yf_ref, sf_ref, s_i < t_i, s_i <= t_i),
        (pb_ref, qb_ref, kb_ref, rb_ref, vb_ref, wrow_b, yb_ref, sb_ref, s_i > t_i, s_i >= t_i),
    )
    chains, sinks = [], []
    for p_ref, q_ref, k_ref, r_ref, v_ref, wrow, y_ref, s_ref, strict, incl in dirs:
        for p in range(npairs):
            sl = slice(p * LANES, (p + 1) * LANES)
            chains.append(dict(p=p_ref[0, :, sl], q=q_ref[0, :, sl], k=k_ref[0, :, sl], r=r_ref[0, :, sl],
                               v=v_ref[0, :, sl], s=s_ref[p], w=wrow[:, sl], strict=strict, incl=incl))
            sinks.append((y_ref, s_ref, p, sl))
    ys, s_news = _chunk_step(chains, blk, eye, m_a, bd)
    for (y_ref, s_ref, p, sl), y, s_new in zip(sinks, ys, s_news):
        y_ref[0, :, sl] = y.astype(y_ref.dtype)
        s_ref[p] = s_new


def _rwkv_scan(v, pf, qf, kf, rf, pb, qb, kb, rb, wc, *, ctx_len):
    b, l, d = v.shape
    c = CHUNK
    nch = l // c
    nctx = ctx_len // c
    per_blk = TOKEN_BLOCK // c

    def bwd_chunk(n):
        return jnp.where(n < nctx, nctx - 1 - n, nch + nctx - 1 - n)

    fwd = pl.BlockSpec((1, c, d), lambda bi, n: (bi, n, 0))
    bwd = pl.BlockSpec((1, c, d), lambda bi, n: (bi, bwd_chunk(n), 0))
    wcf = pl.BlockSpec((1, 1, SUBLANES, d), lambda bi, n: (bi, n // per_blk, 0, 0))
    wcb = pl.BlockSpec((1, 1, SUBLANES, d), lambda bi, n: (bi, bwd_chunk(n) // per_blk, 0, 0))
    npairs = d // LANES
    kern = functools.partial(_rwkv_scan_kernel, bwd_chunk=bwd_chunk)
    return pl.pallas_call(
        kern,
        out_shape=[jax.ShapeDtypeStruct((b, l, d), BF)] * 2,
        grid=(b, nch),
        in_specs=[fwd] * 5 + [wcf] + [bwd] * 5 + [wcb],
        out_specs=[fwd, bwd],
        scratch_shapes=[pltpu.VMEM((npairs, LANES, LANES), F32)] * 2,
        compiler_params=_cparams(("parallel", "arbitrary")),
        name="rwkv_scan",
    )(pf, qf, kf, rf, v, wc, pb, qb, kb, rb, v, wc)


def _rwkv_readout_kernel(yf_ref, yb_ref, bon_ref, g_ref, vec_ref, ones_ref, o_ref):
    ones = ones_ref[...]
    y = yf_ref[...].astype(F32) + yb_ref[...].astype(F32)
    inv_n = 1.0 / RW_HEAD
    yc = y - _head_sums(y, ones, True) * inv_n
    var = _head_sums(yc * yc, ones, True) * inv_n
    yn = yc * lax.rsqrt(var + RW_GN_EPS) * vec_ref[0:1, :] + vec_ref[1:2, :]
    o_ref[...] = ((yn + bon_ref[...].astype(F32)) * g_ref[...].astype(F32)).astype(BF)


def _latent_block(nblocks):
    return lambda i: (i // (nblocks - 1)) * nblocks + 1 + i % (nblocks - 1)


def _rwkv_readout(yf, yb, bon, g, vec, ones, *, seq_len, latent_only):
    m, d = yf.shape
    tm = TOKEN_BLOCK
    nb = seq_len // tm
    tok = pl.BlockSpec((tm, d), lambda i: (i, 0))
    src = pl.BlockSpec((tm, d), lambda i, f=_latent_block(nb): (f(i), 0)) if latent_only else tok
    rows = m // nb * (nb - 1) if latent_only else m
    return pl.pallas_call(
        _rwkv_readout_kernel,
        out_shape=jax.ShapeDtypeStruct((rows, d), BF),
        grid=(rows // tm,),
        in_specs=[src, src, src, src, pl.BlockSpec((SUBLANES, d), lambda i: (0, 0)),
                  pl.BlockSpec((ONES_TILE, ONES_TILE), lambda i: (0, 0))],
        out_specs=tok,
        compiler_params=_cparams(("parallel",)),
        name="rwkv_readout",
    )(yf, yb, bon, g, vec, ones)


CONV_ROWS = 32
CONV_LANES = 512


def _conv_kernel(z_ref, zp_ref, zn_ref, w_ref, vec_ref, o_ref, gp_ref, gs_ref, u_ref, *, nblocks, first_block):
    i = pl.program_id(1) + first_block
    tb = z_ref.shape[1]
    d = D_MODEL
    h = HALO

    def glu(zz):
        zz = zz.astype(F32)
        return zz[:, :d] * _sigmoid(zz[:, d:])

    prev_ok = i >= 2
    next_ok = jnp.logical_and(i >= 1, i <= nblocks - 2)
    gp_ref[0:h, :] = jnp.where(prev_ok, glu(zp_ref[0]), 0.0)
    gp_ref[h:h + tb, :] = glu(z_ref[0])
    gp_ref[h + tb:2 * h + tb, :] = jnp.where(next_ok, glu(zn_ref[0]), 0.0)
    rows = gs_ref.shape[1]
    for s in range(SUBLANES):
        gs_ref[s] = gp_ref[s:s + rows, :]

    off = h - CONV_WIDTH // 2
    grp = CONV_ROWS // SUBLANES
    for r0 in range(0, tb, CONV_ROWS):
        for c0 in range(0, d, CONV_LANES):
            acc = jnp.zeros((grp, SUBLANES, CONV_LANES), F32)
            for k in range(CONV_WIDTH):
                a, s = divmod(k + off, SUBLANES)
                win = gs_ref[s, r0 + a * SUBLANES:r0 + a * SUBLANES + CONV_ROWS, c0:c0 + CONV_LANES]
                acc = acc + win.reshape(grp, SUBLANES, CONV_LANES) * w_ref[k, :, c0:c0 + CONV_LANES][None]
            u_ref[r0:r0 + CONV_ROWS, c0:c0 + CONV_LANES] = acc.reshape(CONV_ROWS, CONV_LANES)

    u = u_ref[...] + vec_ref[0:1, :]
    mu = jnp.mean(u, axis=-1, keepdims=True)
    uc = u - mu
    var = jnp.mean(uc * uc, axis=-1, keepdims=True)
    y = uc * lax.rsqrt(var + EPS) * vec_ref[1:2, :] + vec_ref[2:3, :]
    o_ref[0] = (y * _sigmoid(y)).astype(BF)


def _conv_branch(z_cv, w, vec, *, first_block):
    b, l, cols = z_cv.shape
    tb = TOKEN_BLOCK
    nb = l // tb
    d = D_MODEL
    hb = tb // HALO
    nhalo = l // HALO
    f0 = first_block
    shifted_rows = tb + 2 * HALO - SUBLANES
    kern = functools.partial(_conv_kernel, nblocks=nb, first_block=f0)
    return pl.pallas_call(
        kern,
        out_shape=jax.ShapeDtypeStruct((b, l - f0 * tb, d), BF),
        grid=(b, nb - f0),
        in_specs=[
            pl.BlockSpec((1, tb, cols), lambda bi, i: (bi, i + f0, 0)),
            pl.BlockSpec((1, HALO, cols), lambda bi, i: (bi, jnp.maximum((i + f0) * hb - 1, 0), 0)),
            pl.BlockSpec((1, HALO, cols), lambda bi, i: (bi, jnp.minimum((i + f0 + 1) * hb, nhalo - 1), 0)),
            pl.BlockSpec((CONV_WIDTH, SUBLANES, d), lambda bi, i: (0, 0, 0)),
            pl.BlockSpec((SUBLANES, d), lambda bi, i: (0, 0)),
        ],
        out_specs=pl.BlockSpec((1, tb, d), lambda bi, i: (bi, i, 0)),
        scratch_shapes=[pltpu.VMEM((tb + 2 * HALO, d), F32), pltpu.VMEM((SUBLANES, shifted_rows, d), F32),
                        pltpu.VMEM((tb, d), F32)],
        compiler_params=_cparams(("parallel", "parallel")),
        name="conv_branch",
    )(z_cv, z_cv, z_cv, w, vec)


def _norm_rope(x, g, cos, sin):
    xn = _rms(x.astype(F32), g)
    return xn * cos + pltpu.roll(xn, ATT_HEAD // 2, 1) * sin


def _kv_prep_kernel(z_ref, cos_ref, sin_ref, gk_ref, k_ref, v_ref):
    hd = ATT_HEAD
    for h in range(ATT_KV_HEADS):
        sl = slice(h * hd, (h + 1) * hd)
        k_ref[0, :, sl] = _norm_rope(z_ref[0, :, sl], gk_ref[...], cos_ref[...], sin_ref[...]).astype(BF)
        v_ref[0, :, 2 * h * hd:(2 * h + 1) * hd] = z_ref[0, :, KV_COLS + h * hd:KV_COLS + (h + 1) * hd]
        v_ref[0, :, (2 * h + 1) * hd:(2 * h + 2) * hd] = jnp.ones((z_ref.shape[1], hd), BF)


def _kv_prep(z_at, cos_t, sin_t, gk):
    b, l, _ = z_at.shape
    tb = TOKEN_BLOCK
    tok = lambda bi, i: (bi, i, 0)
    kv_block = D_MODEL // (2 * KV_COLS)
    return pl.pallas_call(
        _kv_prep_kernel,
        out_shape=[jax.ShapeDtypeStruct((b, l, KV_COLS), BF), jax.ShapeDtypeStruct((b, l, 2 * KV_COLS), BF)],
        grid=(b, l // tb),
        in_specs=[
            pl.BlockSpec((1, tb, 2 * KV_COLS), lambda bi, i: (bi, i, kv_block)),
            pl.BlockSpec((tb, ATT_HEAD), lambda bi, i: (i, 0)),
            pl.BlockSpec((tb, ATT_HEAD), lambda bi, i: (i, 0)),
            pl.BlockSpec((1, ATT_HEAD), lambda bi, i: (0, 0)),
        ],
        out_specs=[pl.BlockSpec((1, tb, KV_COLS), tok), pl.BlockSpec((1, tb, 2 * KV_COLS), tok)],
        compiler_params=_cparams(("parallel", "parallel")),
        name="kv_prep",
    )(z_at, cos_t, sin_t, gk)


def _attn_kernel(z_ref, cos_ref, sin_ref, gq_ref, k_ref, v_ref, o_ref, *, ctx_len, first_block):
    hd = ATT_HEAD
    q_scale = (hd ** -0.5) * LOG2_E

    def run(nk):
        s_prev = p_prev = None
        tq = z_ref.shape[1]
        nunits = ATT_Q_HEADS // ATT_STACK
        for un in range(nunits + 2):
            s_new = None
            if un < nunits:
                hk = un * ATT_STACK // ATT_GROUP
                qs = [(_norm_rope(z_ref[0, :, h * hd:(h + 1) * hd], gq_ref[...], cos_ref[...], sin_ref[...]) * q_scale).astype(BF)
                      for h in range(un * ATT_STACK, (un + 1) * ATT_STACK)]
                s_new = _dot_nt(qs[0] if ATT_STACK == 1 else jnp.concatenate(qs, axis=0), k_ref[0, 0:nk, hk * hd:(hk + 1) * hd])
            p_new = None
            if s_prev is not None:
                p_new = jnp.exp2(s_prev - jnp.max(s_prev, axis=-1, keepdims=True)).astype(BF)
            if p_prev is not None:
                uo = un - 2
                hk = uo * ATT_STACK // ATT_GROUP
                o = jnp.dot(p_prev, v_ref[0, 0:nk, 2 * hk * hd:(2 * hk + 2) * hd], preferred_element_type=F32)
                o = (o[:, :hd] / o[:, hd:]).astype(BF)
                for g in range(ATT_STACK):
                    ho = uo * ATT_STACK + g
                    o_ref[0, :, ho * hd:(ho + 1) * hd] = o[g * tq:(g + 1) * tq]
            s_prev, p_prev = s_new, p_new

    if first_block == 0:
        @pl.when(pl.program_id(1) == 0)
        def _():
            run(ctx_len)

        @pl.when(pl.program_id(1) > 0)
        def _():
            run(k_ref.shape[1])
    else:
        run(k_ref.shape[1])


def _attention(z_at, cos_t, sin_t, gq, k, v, *, ctx_len, first_block):
    b, l, _ = z_at.shape
    d = D_MODEL
    tb = TOKEN_BLOCK
    f0 = first_block
    kern = functools.partial(_attn_kernel, ctx_len=ctx_len, first_block=f0)
    seq = lambda bi, i: (bi, 0, 0)
    return pl.pallas_call(
        kern,
        out_shape=jax.ShapeDtypeStruct((b, l - f0 * tb, d), BF),
        grid=(b, l // tb - f0),
        in_specs=[
            pl.BlockSpec((1, tb, d), lambda bi, i: (bi, i + f0, 0)),
            pl.BlockSpec((tb, ATT_HEAD), lambda bi, i: (i + f0, 0)),
            pl.BlockSpec((tb, ATT_HEAD), lambda bi, i: (i + f0, 0)),
            pl.BlockSpec((1, ATT_HEAD), lambda bi, i: (0, 0)),
            pl.BlockSpec((1, l, KV_COLS), seq),
            pl.BlockSpec((1, l, 2 * KV_COLS), seq),
        ],
        out_specs=pl.BlockSpec((1, tb, d), lambda bi, i: (bi, i, 0)),
        compiler_params=_cparams(("parallel", "parallel")),
        name="attention",
    )(z_at, cos_t, sin_t, gq, k, v)


def _merge_kernel(orw_ref, ocv_ref, oat_ref, x_ref, wg_ref, wb_ref, wo_ref, gpre_ref, g_ref, ml_ref, mc_ref, o_ref,
                  *, blocks_per_seq, ctx_len):
    d = D_MODEL
    is_ctx = False if blocks_per_seq is None else _ctx_rows(x_ref.shape[0], blocks_per_seq, ctx_len)
    x = x_ref[...]
    h = (_rms(x, gpre_ref[...]) * (1.0 + _mod_rows(ml_ref, mc_ref, 1, is_ctx)) + _mod_rows(ml_ref, mc_ref, 0, is_ctx)).astype(BF)
    m = None
    for n, o_n in enumerate((orw_ref, ocv_ref, oat_ref)):
        gate = _sigmoid(jnp.dot(h, wg_ref[:, n * d:(n + 1) * d], preferred_element_type=F32))
        t = gate * jnp.dot(o_n[...], wb_ref[n], preferred_element_type=F32)
        m = t if m is None else m + t
    out = _dot(m, wo_ref[...])
    o_ref[...] = x + _mod_rows(ml_ref, mc_ref, 2, is_ctx) * _rms(out, g_ref[...])


def _merge(orw, ocv, oat, x2, wg, wb, wo, gain_pre, gain, modl, modc, *, seq_len, ctx_len, latent_only):
    m, d = x2.shape
    if latent_only:
        tm = TOKEN_BLOCK
        nb = seq_len // tm
        bps = nb - 1
        rows = m // nb * bps
        src = _latent_block(nb)
        kern = functools.partial(_merge_kernel, blocks_per_seq=None, ctx_len=ctx_len)
    else:
        tm = _row_block(seq_len)
        bps = seq_len // tm
        rows = m
        src = lambda i: i
        kern = functools.partial(_merge_kernel, blocks_per_seq=bps, ctx_len=ctx_len)
    tok = pl.BlockSpec((tm, d), lambda i: (i, 0))
    return pl.pallas_call(
        kern,
        out_shape=jax.ShapeDtypeStruct((rows, d), F32),
        grid=(rows // tm,),
        in_specs=[
            tok, tok, tok,
            pl.BlockSpec((tm, d), lambda i: (src(i), 0)),
            _resident(wg.shape),
            _resident(wb.shape),
            _resident(wo.shape),
            pl.BlockSpec((1, d), lambda i: (0, 0)),
            pl.BlockSpec((1, d), lambda i: (0, 0)),
            pl.BlockSpec((1, SUBLANES, d), lambda i: (i // bps, 0, 0)),
            pl.BlockSpec((SUBLANES, d), lambda i: (0, 0)),
        ],
        out_specs=tok,
        compiler_params=_cparams(("parallel",)),
        name="merge",
    )(orw, ocv, oat, x2, wg, wb, wo, gain_pre, gain, modl, modc)


def _mlp_kernel(x_ref, w1_ref, w2_ref, gpre_ref, gpost_ref, ml_ref, mc_ref, o_ref, *, blocks_per_seq, ctx_len):
    is_ctx = False if blocks_per_seq is None else _ctx_rows(x_ref.shape[0], blocks_per_seq, ctx_len)
    x = x_ref[...]
    h = _rms(x, gpre_ref[...]) * (1.0 + _mod_rows(ml_ref, mc_ref, 4, is_ctx)) + _mod_rows(ml_ref, mc_ref, 3, is_ctx)
    hb = h.astype(BF)
    out = None
    for c0 in range(0, w1_ref.shape[1], FF_TILE):
        a = jnp.maximum(jnp.dot(hb, w1_ref[:, c0:c0 + FF_TILE], preferred_element_type=F32), 0.0)
        t = _dot(a * a, w2_ref[c0:c0 + FF_TILE, :])
        out = t if out is None else out + t
    o_ref[...] = x + _mod_rows(ml_ref, mc_ref, 5, is_ctx) * _rms(out, gpost_ref[...])


def _mlp(x2, w1, w2, gpre, gpost, modl, modc, *, seq_len, ctx_len, latent_only):
    m, d = x2.shape
    tm = _row_block(seq_len)
    bps = seq_len // tm
    tok = pl.BlockSpec((tm, d), lambda i: (i, 0))
    kern = functools.partial(_mlp_kernel, blocks_per_seq=None if latent_only else bps, ctx_len=ctx_len)
    return pl.pallas_call(
        kern,
        out_shape=jax.ShapeDtypeStruct((m, d), F32),
        grid=(m // tm,),
        in_specs=[
            tok,
            _resident(w1.shape),
            _resident(w2.shape),
            pl.BlockSpec((1, d), lambda i: (0, 0)),
            pl.BlockSpec((1, d), lambda i: (0, 0)),
            pl.BlockSpec((1, SUBLANES, d), lambda i: (i // bps, 0, 0)),
            pl.BlockSpec((SUBLANES, d), lambda i: (0, 0)),
        ],
        out_specs=tok,
        compiler_params=_cparams(("parallel",)),
        name="mlp",
    )(x2, w1, w2, gpre, gpost, modl, modc)


def _pad_rows(a, rows):
    return jnp.pad(a, ((0, rows - a.shape[0]), (0, 0)))


def _rope_tables(seq, ctx_len):
    rows = seq // GRID_W
    row = jnp.repeat(jnp.arange(rows), GRID_W).astype(F32)
    col = jnp.tile(jnp.arange(GRID_W), rows).astype(F32)
    axis_dim = ATT_HEAD // 2
    freqs = ROPE_THETA ** (-jnp.arange(0, axis_dim, 2, dtype=F32) / axis_dim)
    ang = jnp.concatenate([row[:, None] * freqs, col[:, None] * freqs], axis=-1)
    cos, sin = jnp.cos(ang), jnp.sin(ang)
    cos_t = jnp.concatenate([cos, cos], axis=-1)
    sin_t = jnp.concatenate([-sin, sin], axis=-1)
    cos_t = jnp.concatenate([jnp.ones((ctx_len, ATT_HEAD), F32), cos_t], axis=0)
    sin_t = jnp.concatenate([jnp.zeros((ctx_len, ATT_HEAD), F32), sin_t], axis=0)
    return cos_t, sin_t


def _chunk_tri(upper):
    t = jnp.arange(TOKEN_BLOCK)
    same = (t[:, None] // CHUNK) == (t[None, :] // CHUNK)
    tri = (t[None, :] >= t[:, None]) if upper else (t[None, :] <= t[:, None])
    return jnp.logical_and(same, tri).astype(BF)


def kernel(x, c, ctx, c_ctx, w_mod, b_mod, norm_mix_pre, norm_mix_post, norm_mlp_pre, norm_mlp_post, w_in, rw_mu, rw_w0, rw_w2, rw_a0, rw_a2, rw_g2, rw_k_k, rw_k_a, rw_r_k, rw_ln_g, rw_ln_b, cv_dw_w, cv_dw_b, cv_ln_g, cv_ln_b, at_q_norm, at_k_norm, w_branch, w_out, w_ff1, w_ff2):
    b, s, d = x.shape
    ctx_len = ctx.shape[1]
    depth = w_in.shape[0]
    l = ctx_len + s
    assert d == D_MODEL and ctx_len == TOKEN_BLOCK and s % TOKEN_BLOCK == 0

    mod_rows = -(-(b + 1) // SUBLANES) * SUBLANES
    cvec = _pad_rows(jnp.concatenate([c, c_ctx[None, :]], axis=0), mod_rows)
    mods = _modulation(cvec, w_mod, b_mod).reshape(depth, mod_rows, N_MOD, d)
    mods = jnp.pad(mods, ((0, 0), (0, 0), (0, SUBLANES - N_MOD), (0, 0)))

    cos_t, sin_t = _rope_tables(s, ctx_len)
    half_perm = jnp.concatenate([jnp.arange(0, ATT_HEAD, 2), jnp.arange(1, ATT_HEAD, 2)])
    qk_perm = (jnp.arange(ATT_Q_HEADS + ATT_KV_HEADS)[:, None] * ATT_HEAD + half_perm[None, :]).reshape(-1)
    ones_blk = (jnp.arange(ONES_TILE)[:, None] // RW_HEAD == jnp.arange(ONES_TILE)[None, :] // RW_HEAD).astype(BF)
    trif, trib = _chunk_tri(False), _chunk_tri(True)
    tpos = jnp.arange(TOKEN_BLOCK)
    shift_band = (0.5 * (jnp.abs(tpos[:, None] - tpos[None, :]) == 1) - (tpos[:, None] == tpos[None, :])).astype(BF)
    zpad64 = jnp.zeros((64, d), F32)

    xu = jnp.concatenate([ctx, x], axis=1).reshape(b * l, d)

    for li in range(depth):
        modl, modc = mods[li, :b], mods[li, b]
        w_l = w_in[li]
        c0, c1, c2 = RWKV_COLS, RWKV_COLS + 2 * d, RWKV_COLS + 2 * d + ATT_COLS
        w_rw = w_l[:, :c0].astype(BF)
        w_cv = w_l[:, c0:c1].astype(BF)
        w_at = w_l[:, c1:c2]
        w_at = jnp.concatenate([w_at[:, qk_perm], w_at[:, d + KV_COLS:]], axis=1).astype(BF)
        w_gt = w_l[:, c2:].astype(BF)
        z_rw, z_cv, z_at = _modproj(xu, norm_mix_pre[li][None, :], modl, modc, (w_rw, w_cv, w_at),
                                    (1152, 2048, 1536), seq_len=l, ctx_len=ctx_len, shift_idx=0, scale_idx=1)
        z_rw = z_rw.reshape(b, l, RWKV_COLS)
        z_cv = z_cv.reshape(b, l, 2 * d)
        z_at = z_at.reshape(b, l, ATT_COLS)

        rw_vec = _pad_rows(jnp.stack([rw_w0[li, 0], rw_w0[li, 1], rw_a0[li, 0], rw_a0[li, 1], rw_k_k[li], rw_k_a[li],
                                      rw_r_k[li].reshape(-1)]), SUBLANES)
        w2cat = jnp.concatenate([jnp.concatenate([rw_w2[li, 0], zpad64], axis=0),
                                 jnp.concatenate([zpad64, rw_w2[li, 1]], axis=0)], axis=1).astype(BF)
        a2cat = jnp.concatenate([jnp.concatenate([rw_a2[li, 0], zpad64], axis=0),
                                 jnp.concatenate([zpad64, rw_a2[li, 1]], axis=0)], axis=1).astype(BF)
        v, pf, qf, kf, rf, pb, qb, kb, rb, wc, g, bon = _rwkv_prep(
            z_rw, rw_mu[li][None, :], rw_vec, w2cat, a2cat, rw_g2[li].astype(BF), ones_blk, trif, trib, shift_band)
        yf, yb = _rwkv_scan(v, pf, qf, kf, rf, pb, qb, kb, rb, wc, ctx_len=ctx_len)
        last = li == depth - 1
        first_block = 1 if last else 0
        ln_vec = _pad_rows(jnp.stack([rw_ln_g[li], rw_ln_b[li]]), SUBLANES)
        o_rw = _rwkv_readout(yf.reshape(b * l, d), yb.reshape(b * l, d), bon.reshape(b * l, d), g.reshape(b * l, d),
                             ln_vec, ones_blk, seq_len=l, latent_only=last)

        cv_vec = _pad_rows(jnp.stack([cv_dw_b[li], cv_ln_g[li], cv_ln_b[li]]), SUBLANES)
        dw_w = jnp.broadcast_to(cv_dw_w[li][:, None, :], (CONV_WIDTH, SUBLANES, d))
        o_cv = _conv_branch(z_cv, dw_w, cv_vec, first_block=first_block).reshape(-1, d)

        k, vv = _kv_prep(z_at, cos_t, sin_t, at_k_norm[li][half_perm][None, :])
        o_at = _attention(z_at, cos_t, sin_t, at_q_norm[li][half_perm][None, :], k, vv,
                          ctx_len=ctx_len, first_block=first_block).reshape(-1, d)

        xu = _merge(o_rw, o_cv, o_at, xu, w_gt, w_branch[li].astype(BF), w_out[li].astype(BF),
                    norm_mix_pre[li][None, :], norm_mix_post[li][None, :], modl, modc,
                    seq_len=l, ctx_len=ctx_len, latent_only=last)
        xu = _mlp(xu, w_ff1[li].astype(BF), w_ff2[li].astype(BF), norm_mlp_pre[li][None, :], norm_mlp_post[li][None, :],
                  modl, modc, seq_len=s if last else l, ctx_len=ctx_len, latent_only=last)

    return xu.reshape(b, s, d)
```

```python
import functools
import math

import jax
import jax.numpy as jnp
from jax import lax
from jax.experimental import pallas as pl
from jax.experimental.pallas import tpu as pltpu

F32 = jnp.float32
BF = jnp.bfloat16

D_MODEL = 1024
N_MOD = 6
EPS = 1e-6
RW_HEAD = 64
RW_HEADS = 16
RW_GN_EPS = 64e-5
RW_LORA_COLS = 384
RWKV_COLS = 3 * D_MODEL + RW_LORA_COLS
CONV_WIDTH = 31
HALO = 16
ONES_TILE = 256
ATT_HEAD = 128
ATT_Q_HEADS = 8
ATT_KV_HEADS = 2
ATT_GROUP = ATT_Q_HEADS // ATT_KV_HEADS
ATT_STACK = 2
KV_COLS = ATT_KV_HEADS * ATT_HEAD
ATT_COLS = D_MODEL + 2 * KV_COLS
GRID_W = 64
ROPE_THETA = 10000.0
D_FF = 4 * D_MODEL

LANES = 128
SUBLANES = 8
CHUNK = 64
TOKEN_BLOCK = 256
PROJ_ROWS = 768
FF_TILE = 1024
VMEM_LIMIT = 56 * 1024 * 1024
EXP_NEG_HALF = math.exp(-0.5)
LOG2_E = math.log2(math.e)


def _cparams(sem):
    return pltpu.CompilerParams(dimension_semantics=sem, vmem_limit_bytes=VMEM_LIMIT)


def _dot(a, b):
    return jnp.dot(a.astype(BF), b.astype(BF), preferred_element_type=F32)


def _dot_nt(a, b):
    return lax.dot_general(a.astype(BF), b.astype(BF), (((1,), (1,)), ((), ())), preferred_element_type=F32)


def _dot_tn(a, b):
    return lax.dot_general(a.astype(BF), b.astype(BF), (((0,), (0,)), ((), ())), preferred_element_type=F32)


def _split_dot(x, w):
    hi = x.astype(BF)
    lo = (x - hi.astype(F32)).astype(BF)
    return jnp.dot(hi, w, preferred_element_type=F32) + jnp.dot(lo, w, preferred_element_type=F32)


def _head_sums(x, ones, split):
    w = ones.shape[0]
    dot = _split_dot if split else _dot
    return jnp.concatenate([dot(x[:, n * w:(n + 1) * w], ones) for n in range(x.shape[1] // w)], axis=1)


def _sigmoid(x):
    return 1.0 / (1.0 + jnp.exp(-x))


def _rms(x, g):
    return x * lax.rsqrt(jnp.mean(x * x, axis=-1, keepdims=True) + EPS) * g


def _mod_kernel(c_ref, w_ref, b_ref, o_ref):
    c = c_ref[...]
    o_ref[0] = _dot(c * _sigmoid(c), w_ref[0]) + b_ref[0]


def _modulation(cvec, w_mod, b_mod):
    depth, d, n = w_mod.shape
    rows = cvec.shape[0]
    tn = 1536
    return pl.pallas_call(
        _mod_kernel,
        out_shape=jax.ShapeDtypeStruct((depth, rows, n), F32),
        grid=(depth, n // tn),
        in_specs=[
            pl.BlockSpec((rows, d), lambda l, j: (0, 0)),
            pl.BlockSpec((1, d, tn), lambda l, j: (l, 0, j)),
            pl.BlockSpec((1, 1, tn), lambda l, j: (l, 0, j)),
        ],
        out_specs=pl.BlockSpec((1, rows, tn), lambda l, j: (l, 0, j)),
        compiler_params=_cparams(("parallel", "parallel")),
        name="modulation",
    )(cvec, w_mod, b_mod.reshape(depth, 1, n))


def _mod_rows(ml_ref, mc_ref, idx, is_ctx):
    return jnp.where(is_ctx, mc_ref[idx:idx + 1, :], ml_ref[0, idx:idx + 1, :])


def _ctx_rows(tm, blocks_per_seq, ctx_len):
    row = (pl.program_id(0) % blocks_per_seq) * tm + lax.broadcasted_iota(jnp.int32, (tm, 1), 0)
    return row < ctx_len


def _row_block(seq_len):
    return max(t for t in range(LANES, PROJ_ROWS + 1, LANES) if seq_len % t == 0)


def _resident(shape):
    return pl.BlockSpec(shape, lambda *_: (0,) * len(shape), pipeline_mode=pl.Buffered(1))


def _modproj_kernel(x_ref, g_ref, ml_ref, mc_ref, *rest, starts, shift_idx, scale_idx, blocks_per_seq, ctx_len):
    ngroups = len(starts) - 1
    w_refs, o_refs, h_ref = rest[:ngroups], rest[ngroups:2 * ngroups], rest[2 * ngroups]
    j = pl.program_id(1)

    @pl.when(j == 0)
    def _():
        xn = _rms(x_ref[...], g_ref[...])
        is_ctx = _ctx_rows(x_ref.shape[0], blocks_per_seq, ctx_len)
        scale = _mod_rows(ml_ref, mc_ref, scale_idx, is_ctx)
        shift = _mod_rows(ml_ref, mc_ref, shift_idx, is_ctx)
        h_ref[...] = (xn * (1.0 + scale) + shift).astype(BF)

    for k in range(ngroups):
        tn = o_refs[k].shape[1]
        for t in range(starts[k + 1] - starts[k]):
            @pl.when(j == starts[k] + t)
            def _(k=k, t=t, tn=tn):
                o_refs[k][...] = jnp.dot(h_ref[...], w_refs[k][:, t * tn:(t + 1) * tn],
                                         preferred_element_type=F32).astype(o_refs[k].dtype)


def _modproj(x2, gain, modl, modc, weights, tiles, *, seq_len, ctx_len, shift_idx, scale_idx):
    m, d = x2.shape
    tm = _row_block(seq_len)
    bps = seq_len // tm
    nblk = [w.shape[1] // tn for w, tn in zip(weights, tiles)]
    starts = [0]
    for nb in nblk:
        starts.append(starts[-1] + nb)

    def col(k):
        return lambda j: jnp.clip(j - starts[k], 0, nblk[k] - 1)

    kern = functools.partial(_modproj_kernel, starts=tuple(starts), shift_idx=shift_idx, scale_idx=scale_idx,
                             blocks_per_seq=bps, ctx_len=ctx_len)
    w_specs = [_resident(w.shape) for w in weights]
    o_specs = [pl.BlockSpec((tm, tn), lambda i, j, c=col(k): (i, c(j))) for k, tn in enumerate(tiles)]
    return pl.pallas_call(
        kern,
        out_shape=[jax.ShapeDtypeStruct((m, w.shape[1]), BF) for w in weights],
        grid=(m // tm, starts[-1]),
        in_specs=[
            pl.BlockSpec((tm, d), lambda i, j: (i, 0)),
            pl.BlockSpec((1, d), lambda i, j: (0, 0)),
            pl.BlockSpec((1, SUBLANES, d), lambda i, j: (i // bps, 0, 0)),
            pl.BlockSpec((SUBLANES, d), lambda i, j: (0, 0)),
        ] + w_specs,
        out_specs=o_specs,
        scratch_shapes=[pltpu.VMEM((tm, d), BF)],
        compiler_params=_cparams(("parallel", "arbitrary")),
        name="modproj",
    )(x2, gain, modl, modc, *weights)


def _rwkv_prep_kernel(z_ref, zp_ref, zn_ref, mu_ref, vec_ref, w2_ref, a2_ref, g2_ref, ones_ref, trif_ref, trib_ref, band_ref,
                      v_ref, pf_ref, qf_ref, kf_ref, rf_ref, pb_ref, qb_ref, kb_ref, rb_ref, wc_ref, g_ref, bon_ref,
                      *, nblocks):
    i = pl.program_id(1)
    zb = z_ref[0]
    z = zb.astype(F32)
    tb = z.shape[0]
    prev_ok = i >= 2
    next_ok = jnp.logical_and(i >= 1, i <= nblocks - 2)
    zp_row = jnp.where(prev_ok, zp_ref[0, HALO - 1:HALO, :].astype(F32), 0.0)
    zn_row = jnp.where(next_ok, zn_ref[0, 0:1, :].astype(F32), 0.0)
    delta = jnp.dot(band_ref[...], zb, preferred_element_type=F32)
    sub = lax.broadcasted_iota(jnp.int32, (SUBLANES, 1), 0)
    top = delta[0:SUBLANES] + jnp.where(sub == 0, 0.5 * zp_row, 0.0)
    bot = delta[tb - SUBLANES:tb] + jnp.where(sub == SUBLANES - 1, 0.5 * zn_row, 0.0)
    delta = jnp.concatenate([top, delta[SUBLANES:tb - SUBLANES], bot], axis=0)
    zs = z + mu_ref[...] * delta

    d = D_MODEL
    r = zs[:, 0:d]
    k = zs[:, d:2 * d]
    v = zs[:, 2 * d:3 * d]
    w_lo = _dot(jnp.tanh(zs[:, 3 * d:3 * d + 128]), w2_ref[...])
    a_lo = _dot(zs[:, 3 * d + 128:3 * d + 256], a2_ref[...])
    g_ref[0] = _dot(_sigmoid(zs[:, 3 * d + 256:3 * d + 384]), g2_ref[...]).astype(BF)

    vec = vec_ref[...]
    ones = ones_ref[...]
    kk0 = k * vec[4:5]
    kk = kk0 * jnp.minimum(lax.rsqrt(_head_sums(kk0 * kk0, ones, True)), 1e12)

    keysum = None
    outs = ((pf_ref, qf_ref, kf_ref, rf_ref, trif_ref), (pb_ref, qb_ref, kb_ref, rb_ref, trib_ref))
    for dr, (p_ref, q_ref, k_ref, r_ref, tri_ref) in enumerate(outs):
        w_raw = vec[dr:dr + 1] + w_lo[:, dr * d:(dr + 1) * d]
        lw = -EXP_NEG_HALF * _sigmoid(w_raw)
        a = _sigmoid(vec[2 + dr:3 + dr] + a_lo[:, dr * d:(dr + 1) * d])
        key = k * (1.0 + (a - 1.0) * vec[5:6])
        cum = _split_dot_left(tri_ref[...], lw)
        wt = jnp.exp(cum)
        iw = jnp.exp(-cum)
        p_ref[0] = (kk * jnp.exp(cum - lw)).astype(BF)
        q_ref[0] = (a * kk * iw).astype(BF)
        k_ref[0] = (key * iw).astype(BF)
        r_ref[0] = (r * wt).astype(BF)
        for c in range(tb // CHUNK):
            last = c * CHUNK + (CHUNK - 1 if dr == 0 else 0)
            wc_ref[0, 0, dr * 4 + c:dr * 4 + c + 1, :] = wt[last:last + 1, :]
        keysum = key if keysum is None else keysum + key

    bon_ref[0] = (_head_sums(r * keysum * vec[6:7], ones, False) * v).astype(BF)
    v_ref[0] = v.astype(BF)


def _split_dot_left(w, x):
    hi = x.astype(BF)
    lo = (x - hi.astype(F32)).astype(BF)
    return jnp.dot(w, hi, preferred_element_type=F32) + jnp.dot(w, lo, preferred_element_type=F32)


def _rwkv_prep(z_rw, mu, vec, w2cat, a2cat, g2, ones, trif, trib, band):
    b, l, cols = z_rw.shape
    tb = TOKEN_BLOCK
    nb = l // tb
    d = D_MODEL
    hb = tb // HALO
    nhalo = l // HALO
    tok = lambda bi, i: (bi, i, 0)
    const2 = lambda bi, i: (0, 0)
    seq_bf = jax.ShapeDtypeStruct((b, l, d), BF)
    out_tok = pl.BlockSpec((1, tb, d), tok)
    kern = functools.partial(_rwkv_prep_kernel, nblocks=nb)
    return pl.pallas_call(
        kern,
        out_shape=[seq_bf] * 9 + [jax.ShapeDtypeStruct((b, nb, SUBLANES, d), F32), seq_bf, seq_bf],
        grid=(b, nb),
        in_specs=[
            pl.BlockSpec((1, tb, cols), tok),
            pl.BlockSpec((1, HALO, cols), lambda bi, i: (bi, jnp.maximum(i * hb - 1, 0), 0)),
            pl.BlockSpec((1, HALO, cols), lambda bi, i: (bi, jnp.minimum((i + 1) * hb, nhalo - 1), 0)),
            pl.BlockSpec((1, cols), const2),
            pl.BlockSpec((SUBLANES, d), const2),
            pl.BlockSpec((LANES, 2 * d), const2),
            pl.BlockSpec((LANES, 2 * d), const2),
            pl.BlockSpec((LANES, d), const2),
            pl.BlockSpec((ONES_TILE, ONES_TILE), const2),
            pl.BlockSpec((tb, tb), const2),
            pl.BlockSpec((tb, tb), const2),
            pl.BlockSpec((tb, tb), const2),
        ],
        out_specs=[out_tok] * 9 + [pl.BlockSpec((1, 1, SUBLANES, d), lambda bi, i: (bi, i, 0, 0)), out_tok, out_tok],
        compiler_params=_cparams(("parallel", "parallel")),
        name="rwkv_prep",
    )(z_rw, z_rw, z_rw, mu, vec, w2cat, a2cat, g2, ones, trif, trib, band)


def _sm(x, m_a):
    return jnp.concatenate([jnp.where(m_a, x, 0), jnp.where(m_a, 0, x)], axis=0).astype(BF)


def _chunk_step(chains, levels, eye, m_a, bd):
    c = CHUNK
    sm = lambda x: _sm(x, m_a)
    pm = lambda x, y: jnp.dot(x.astype(BF), sm(y), preferred_element_type=F32)
    each = lambda fn, *cols: [fn(*args) for args in zip(*cols)]

    strict = [ch["strict"] for ch in chains]
    incl = [ch["incl"] for ch in chains]
    p2, q2, k2, r2, v2, s2 = ([ch[n] for ch in chains] for n in ("p", "q", "k", "r", "v", "s"))
    pr = each(lambda p, r: jnp.concatenate([p, r], axis=0), p2, r2)
    gps = each(lambda a, q, k, s: _dot_nt(a, jnp.concatenate([sm(q), sm(k), s.astype(BF)], axis=0)), pr, q2, k2, s2)
    gram = each(lambda g: g[:, :4 * c], gps)
    prs = each(lambda g: g[:, 4 * c:], gps)
    lm = each(lambda g, m: jnp.where(m, g[:c, :2 * c], 0.0), gram, strict)
    rhs = each(lambda g, m, v, ps: ps[:c] + pm(jnp.where(m, g[:c, 2 * c:], 0.0), v), gram, strict, v2, prs)
    tinv = each(lambda l: eye - jnp.where(levels[0], l, 0.0), lm)
    for lvl in levels[1:]:
        tc = each(lambda t, l: pm(t, jnp.where(lvl, l, 0.0)), tinv, lm)
        tinv = each(lambda t, x: t - pm(x, t), tinv, tc)
    u = each(lambda t, r: -pm(t, r), tinv, rhs)
    y = each(lambda g, m, ps, uu, v: ps[c:] + _dot(
        jnp.concatenate([jnp.where(m, g[c:, :2 * c], 0.0), jnp.where(m, g[c:, 2 * c:], 0.0)], axis=1),
        jnp.concatenate([sm(uu), sm(v)], axis=0)), gram, incl, prs, u, v2)
    upd = each(lambda uu, v, q, k: _dot_tn(jnp.concatenate([uu.astype(BF), v], axis=0), jnp.concatenate([q, k], axis=0)),
               u, v2, q2, k2)
    s_new = each(lambda s, up, ch: jnp.where(bd, s + up, 0.0) * ch["w"], s2, upd, chains)
    return y, s_new


def _rwkv_scan_kernel(pf_ref, qf_ref, kf_ref, rf_ref, vf_ref, wcf_ref, pb_ref, qb_ref, kb_ref, rb_ref, vb_ref, wcb_ref,
                      yf_ref, yb_ref, sf_ref, sb_ref, *, bwd_chunk):
    n = pl.program_id(1)
    c = CHUNK
    npairs = sf_ref.shape[0]

    @pl.when(n == 0)
    def _():
        sf_ref[...] = jnp.zeros_like(sf_ref)
        sb_ref[...] = jnp.zeros_like(sb_ref)

    t_i = lax.broadcasted_iota(jnp.int32, (c, 2 * c), 0)
    s_i = jnp.bitwise_and(lax.broadcasted_iota(jnp.int32, (c, 2 * c), 1), c - 1)
    m_a = lax.broadcasted_iota(jnp.int32, (1, LANES), 1) < RW_HEAD
    levels = [jnp.logical_and(t_i // (2 * hb) == s_i // (2 * hb), t_i // hb != s_i // hb)
              for hb in (2 ** k for k in range(c.bit_length() - 1))]
    eye = (t_i == s_i).astype(F32)
    bd = (lax.broadcasted_iota(jnp.int32, (LANES, LANES), 0) // RW_HEAD) == \
         (lax.broadcasted_iota(jnp.int32, (LANES, LANES), 1) // RW_HEAD)
    per_blk = TOKEN_BLOCK // c
    wrow_f = wcf_ref[0, 0, pl.ds(n % per_blk, 1), :]
    wrow_b = wcb_ref[0, 0, pl.ds(per_blk + bwd_chunk(n) % per_blk, 1), :]

    dirs = (
        (pf_ref, qf_ref, kf_ref, rf_ref, vf_ref, wrow_f, yf_ref, sf_ref, s_i < t_i, s_i <= t_i),
        (pb_ref, qb_ref, kb_ref, rb_ref, vb_ref, wrow_b, yb_ref, sb_ref, s_i > t_i, s_i >= t_i),
    )
    chains, sinks = [], []
    for p_ref, q_ref, k_ref, r_ref, v_ref, wrow, y_ref, s_ref, strict, incl in dirs:
        for p in range(npairs):
            sl = slice(p * LANES, (p + 1) * LANES)
            chains.append(dict(p=p_ref[0, :, sl], q=q_ref[0, :, sl], k=k_ref[0, :, sl], r=r_ref[0, :, sl],
                               v=v_ref[0, :, sl], s=s_ref[p], w=wrow[:, sl], strict=strict, incl=incl))
            sinks.append((y_ref, s_ref, p, sl))
    ys, s_news = _chunk_step(chains, levels, eye, m_a, bd)
    for (y_ref, s_ref, p, sl), y, s_new in zip(sinks, ys, s_news):
        y_ref[0, :, sl] = y.astype(y_ref.dtype)
        s_ref[p] = s_new


def _rwkv_scan(v, pf, qf, kf, rf, pb, qb, kb, rb, wc, *, ctx_len):
    b, l, d = v.shape
    c = CHUNK
    nch = l // c
    nctx = ctx_len // c
    per_blk = TOKEN_BLOCK // c

    def bwd_chunk(n):
        return jnp.where(n < nctx, nctx - 1 - n, nch + nctx - 1 - n)

    fwd = pl.BlockSpec((1, c, d), lambda bi, n: (bi, n, 0))
    bwd = pl.BlockSpec((1, c, d), lambda bi, n: (bi, bwd_chunk(n), 0))
    wcf = pl.BlockSpec((1, 1, SUBLANES, d), lambda bi, n: (bi, n // per_blk, 0, 0))
    wcb = pl.BlockSpec((1, 1, SUBLANES, d), lambda bi, n: (bi, bwd_chunk(n) // per_blk, 0, 0))
    npairs = d // LANES
    kern = functools.partial(_rwkv_scan_kernel, bwd_chunk=bwd_chunk)
    return pl.pallas_call(
        kern,
        out_shape=[jax.ShapeDtypeStruct((b, l, d), BF)] * 2,
        grid=(b, nch),
        in_specs=[fwd] * 5 + [wcf] + [bwd] * 5 + [wcb],
        out_specs=[fwd, bwd],
        scratch_shapes=[pltpu.VMEM((npairs, LANES, LANES), F32)] * 2,
        compiler_params=_cparams(("parallel", "arbitrary")),
        name="rwkv_scan",
    )(pf, qf, kf, rf, v, wc, pb, qb, kb, rb, v, wc)


def _rwkv_readout_kernel(yf_ref, yb_ref, bon_ref, g_ref, vec_ref, ones_ref, o_ref):
    ones = ones_ref[...]
    y = yf_ref[...].astype(F32) + yb_ref[...].astype(F32)
    inv_n = 1.0 / RW_HEAD
    yc = y - _head_sums(y, ones, True) * inv_n
    var = _head_sums(yc * yc, ones, True) * inv_n
    yn = yc * lax.rsqrt(var + RW_GN_EPS) * vec_ref[0:1, :] + vec_ref[1:2, :]
    o_ref[...] = ((yn + bon_ref[...].astype(F32)) * g_ref[...].astype(F32)).astype(BF)


def _latent_block(nblocks):
    return lambda i: (i // (nblocks - 1)) * nblocks + 1 + i % (nblocks - 1)


def _rwkv_readout(yf, yb, bon, g, vec, ones, *, seq_len, latent_only):
    m, d = yf.shape
    tm = TOKEN_BLOCK
    nb = seq_len // tm
    tok = pl.BlockSpec((tm, d), lambda i: (i, 0))
    src = pl.BlockSpec((tm, d), lambda i, f=_latent_block(nb): (f(i), 0)) if latent_only else tok
    rows = m // nb * (nb - 1) if latent_only else m
    return pl.pallas_call(
        _rwkv_readout_kernel,
        out_shape=jax.ShapeDtypeStruct((rows, d), BF),
        grid=(rows // tm,),
        in_specs=[src, src, src, src, pl.BlockSpec((SUBLANES, d), lambda i: (0, 0)),
                  pl.BlockSpec((ONES_TILE, ONES_TILE), lambda i: (0, 0))],
        out_specs=tok,
        compiler_params=_cparams(("parallel",)),
        name="rwkv_readout",
    )(yf, yb, bon, g, vec, ones)


CONV_ROWS = 32
CONV_LANES = 512


def _conv_kernel(z_ref, zp_ref, zn_ref, w_ref, vec_ref, o_ref, gp_ref, gs_ref, u_ref, *, nblocks, first_block):
    i = pl.program_id(1) + first_block
    tb = z_ref.shape[1]
    d = D_MODEL
    h = HALO

    def glu(zz):
        zz = zz.astype(F32)
        return zz[:, :d] * _sigmoid(zz[:, d:])

    prev_ok = i >= 2
    next_ok = jnp.logical_and(i >= 1, i <= nblocks - 2)
    gp_ref[0:h, :] = jnp.where(prev_ok, glu(zp_ref[0]), 0.0)
    gp_ref[h:h + tb, :] = glu(z_ref[0])
    gp_ref[h + tb:2 * h + tb, :] = jnp.where(next_ok, glu(zn_ref[0]), 0.0)
    rows = gs_ref.shape[1]
    for s in range(SUBLANES):
        gs_ref[s] = gp_ref[s:s + rows, :]

    off = h - CONV_WIDTH // 2
    grp = CONV_ROWS // SUBLANES
    for r0 in range(0, tb, CONV_ROWS):
        for c0 in range(0, d, CONV_LANES):
            acc = jnp.zeros((grp, SUBLANES, CONV_LANES), F32)
            for k in range(CONV_WIDTH):
                a, s = divmod(k + off, SUBLANES)
                win = gs_ref[s, r0 + a * SUBLANES:r0 + a * SUBLANES + CONV_ROWS, c0:c0 + CONV_LANES]
                acc = acc + win.reshape(grp, SUBLANES, CONV_LANES) * w_ref[k, :, c0:c0 + CONV_LANES][None]
            u_ref[r0:r0 + CONV_ROWS, c0:c0 + CONV_LANES] = acc.reshape(CONV_ROWS, CONV_LANES)

    u = u_ref[...] + vec_ref[0:1, :]
    mu = jnp.mean(u, axis=-1, keepdims=True)
    uc = u - mu
    var = jnp.mean(uc * uc, axis=-1, keepdims=True)
    y = uc * lax.rsqrt(var + EPS) * vec_ref[1:2, :] + vec_ref[2:3, :]
    o_ref[0] = (y * _sigmoid(y)).astype(BF)


def _conv_branch(z_cv, w, vec, *, first_block):
    b, l, cols = z_cv.shape
    tb = TOKEN_BLOCK
    nb = l // tb
    d = D_MODEL
    hb = tb // HALO
    nhalo = l // HALO
    f0 = first_block
    shifted_rows = tb + 2 * HALO - SUBLANES
    kern = functools.partial(_conv_kernel, nblocks=nb, first_block=f0)
    return pl.pallas_call(
        kern,
        out_shape=jax.ShapeDtypeStruct((b, l - f0 * tb, d), BF),
        grid=(b, nb - f0),
        in_specs=[
            pl.BlockSpec((1, tb, cols), lambda bi, i: (bi, i + f0, 0)),
            pl.BlockSpec((1, HALO, cols), lambda bi, i: (bi, jnp.maximum((i + f0) * hb - 1, 0), 0)),
            pl.BlockSpec((1, HALO, cols), lambda bi, i: (bi, jnp.minimum((i + f0 + 1) * hb, nhalo - 1), 0)),
            pl.BlockSpec((CONV_WIDTH, SUBLANES, d), lambda bi, i: (0, 0, 0)),
            pl.BlockSpec((SUBLANES, d), lambda bi, i: (0, 0)),
        ],
        out_specs=pl.BlockSpec((1, tb, d), lambda bi, i: (bi, i, 0)),
        scratch_shapes=[pltpu.VMEM((tb + 2 * HALO, d), F32), pltpu.VMEM((SUBLANES, shifted_rows, d), F32),
                        pltpu.VMEM((tb, d), F32)],
        compiler_params=_cparams(("parallel", "parallel")),
        name="conv_branch",
    )(z_cv, z_cv, z_cv, w, vec)


def _norm_rope(x, g, cos, sin):
    xn = _rms(x.astype(F32), g)
    return xn * cos + pltpu.roll(xn, ATT_HEAD // 2, 1) * sin


def _att_prep_kernel(z_ref, cos_ref, sin_ref, gq_ref, gk_ref, q_ref, k_ref, v_ref):
    hd = ATT_HEAD
    cos = cos_ref[...]
    sin = sin_ref[...]
    q_scale = (hd ** -0.5) * LOG2_E
    for h in range(ATT_Q_HEADS):
        sl = slice(h * hd, (h + 1) * hd)
        q_ref[0, :, sl] = (_norm_rope(z_ref[0, :, sl], gq_ref[...], cos, sin) * q_scale).astype(BF)
    for h in range(ATT_KV_HEADS):
        sl = slice(h * hd, (h + 1) * hd)
        k_ref[0, :, sl] = _norm_rope(z_ref[0, :, D_MODEL + h * hd:D_MODEL + (h + 1) * hd], gk_ref[...], cos, sin).astype(BF)
        v_ref[0, :, 2 * h * hd:(2 * h + 1) * hd] = z_ref[0, :, D_MODEL + KV_COLS + h * hd:D_MODEL + KV_COLS + (h + 1) * hd]
        v_ref[0, :, (2 * h + 1) * hd:(2 * h + 2) * hd] = jnp.ones((z_ref.shape[1], hd), BF)


def _att_prep(z_at, cos_t, sin_t, gq, gk):
    b, l, cols = z_at.shape
    tb = TOKEN_BLOCK
    tok = lambda bi, i: (bi, i, 0)
    return pl.pallas_call(
        _att_prep_kernel,
        out_shape=[jax.ShapeDtypeStruct((b, l, D_MODEL), BF), jax.ShapeDtypeStruct((b, l, KV_COLS), BF),
                   jax.ShapeDtypeStruct((b, l, 2 * KV_COLS), BF)],
        grid=(b, l // tb),
        in_specs=[
            pl.BlockSpec((1, tb, cols), tok),
            pl.BlockSpec((tb, ATT_HEAD), lambda bi, i: (i, 0)),
            pl.BlockSpec((tb, ATT_HEAD), lambda bi, i: (i, 0)),
            pl.BlockSpec((1, ATT_HEAD), lambda bi, i: (0, 0)),
            pl.BlockSpec((1, ATT_HEAD), lambda bi, i: (0, 0)),
        ],
        out_specs=[pl.BlockSpec((1, tb, D_MODEL), tok), pl.BlockSpec((1, tb, KV_COLS), tok),
                   pl.BlockSpec((1, tb, 2 * KV_COLS), tok)],
        compiler_params=_cparams(("parallel", "parallel")),
        name="att_prep",
    )(z_at, cos_t, sin_t, gq, gk)


def _attn_kernel(q_ref, k_ref, v_ref, o_ref, *, ctx_len, first_block):
    hd = ATT_HEAD

    def run(nk):
        s_prev = p_prev = None
        tq = q_ref.shape[1]
        nunits = ATT_Q_HEADS // ATT_STACK
        for un in range(nunits + 2):
            s_new = None
            if un < nunits:
                hk = un * ATT_STACK // ATT_GROUP
                qs = [q_ref[0, :, h * hd:(h + 1) * hd] for h in range(un * ATT_STACK, (un + 1) * ATT_STACK)]
                s_new = _dot_nt(qs[0] if ATT_STACK == 1 else jnp.concatenate(qs, axis=0), k_ref[0, 0:nk, hk * hd:(hk + 1) * hd])
            p_new = None
            if s_prev is not None:
                p_new = jnp.exp2(s_prev - jnp.max(s_prev, axis=-1, keepdims=True)).astype(BF)
            if p_prev is not None:
                uo = un - 2
                hk = uo * ATT_STACK // ATT_GROUP
                o = jnp.dot(p_prev, v_ref[0, 0:nk, 2 * hk * hd:(2 * hk + 2) * hd], preferred_element_type=F32)
                o = (o[:, :hd] / o[:, hd:]).astype(BF)
                for g in range(ATT_STACK):
                    ho = uo * ATT_STACK + g
                    o_ref[0, :, ho * hd:(ho + 1) * hd] = o[g * tq:(g + 1) * tq]
            s_prev, p_prev = s_new, p_new

    if first_block == 0:
        @pl.when(pl.program_id(1) == 0)
        def _():
            run(ctx_len)

        @pl.when(pl.program_id(1) > 0)
        def _():
            run(k_ref.shape[1])
    else:
        run(k_ref.shape[1])


def _attention(q, k, v, *, ctx_len, first_block):
    b, l, d = q.shape
    tb = TOKEN_BLOCK
    f0 = first_block
    kern = functools.partial(_attn_kernel, ctx_len=ctx_len, first_block=f0)
    seq = lambda bi, i: (bi, 0, 0)
    return pl.pallas_call(
        kern,
        out_shape=jax.ShapeDtypeStruct((b, l - f0 * tb, d), BF),
        grid=(b, l // tb - f0),
        in_specs=[
            pl.BlockSpec((1, tb, d), lambda bi, i: (bi, i + f0, 0)),
            pl.BlockSpec((1, l, KV_COLS), seq),
            pl.BlockSpec((1, l, 2 * KV_COLS), seq),
        ],
        out_specs=pl.BlockSpec((1, tb, d), lambda bi, i: (bi, i, 0)),
        compiler_params=_cparams(("parallel", "parallel")),
        name="attention",
    )(q, k, v)


def _merge_kernel(orw_ref, ocv_ref, oat_ref, x_ref, wg_ref, wb_ref, wo_ref, gpre_ref, g_ref, ml_ref, mc_ref, o_ref,
                  *, blocks_per_seq, ctx_len):
    d = D_MODEL
    is_ctx = False if blocks_per_seq is None else _ctx_rows(x_ref.shape[0], blocks_per_seq, ctx_len)
    x = x_ref[...]
    h = (_rms(x, gpre_ref[...]) * (1.0 + _mod_rows(ml_ref, mc_ref, 1, is_ctx)) + _mod_rows(ml_ref, mc_ref, 0, is_ctx)).astype(BF)
    m = None
    for n, o_n in enumerate((orw_ref, ocv_ref, oat_ref)):
        gate = _sigmoid(jnp.dot(h, wg_ref[:, n * d:(n + 1) * d], preferred_element_type=F32))
        t = gate * jnp.dot(o_n[...], wb_ref[n], preferred_element_type=F32)
        m = t if m is None else m + t
    out = _dot(m, wo_ref[...])
    o_ref[...] = x + _mod_rows(ml_ref, mc_ref, 2, is_ctx) * _rms(out, g_ref[...])


def _merge(orw, ocv, oat, x2, wg, wb, wo, gain_pre, gain, modl, modc, *, seq_len, ctx_len, latent_only):
    m, d = x2.shape
    if latent_only:
        tm = TOKEN_BLOCK
        nb = seq_len // tm
        bps = nb - 1
        rows = m // nb * bps
        src = _latent_block(nb)
        kern = functools.partial(_merge_kernel, blocks_per_seq=None, ctx_len=ctx_len)
    else:
        tm = _row_block(seq_len)
        bps = seq_len // tm
        rows = m
        src = lambda i: i
        kern = functools.partial(_merge_kernel, blocks_per_seq=bps, ctx_len=ctx_len)
    tok = pl.BlockSpec((tm, d), lambda i: (i, 0))
    return pl.pallas_call(
        kern,
        out_shape=jax.ShapeDtypeStruct((rows, d), F32),
        grid=(rows // tm,),
        in_specs=[
            tok, tok, tok,
            pl.BlockSpec((tm, d), lambda i: (src(i), 0)),
            _resident(wg.shape),
            _resident(wb.shape),
            _resident(wo.shape),
            pl.BlockSpec((1, d), lambda i: (0, 0)),
            pl.BlockSpec((1, d), lambda i: (0, 0)),
            pl.BlockSpec((1, SUBLANES, d), lambda i: (i // bps, 0, 0)),
            pl.BlockSpec((SUBLANES, d), lambda i: (0, 0)),
        ],
        out_specs=tok,
        compiler_params=_cparams(("parallel",)),
        name="merge",
    )(orw, ocv, oat, x2, wg, wb, wo, gain_pre, gain, modl, modc)


def _mlp_kernel(x_ref, w1_ref, w2_ref, gpre_ref, gpost_ref, ml_ref, mc_ref, o_ref, *, blocks_per_seq, ctx_len):
    is_ctx = False if blocks_per_seq is None else _ctx_rows(x_ref.shape[0], blocks_per_seq, ctx_len)
    x = x_ref[...]
    h = _rms(x, gpre_ref[...]) * (1.0 + _mod_rows(ml_ref, mc_ref, 4, is_ctx)) + _mod_rows(ml_ref, mc_ref, 3, is_ctx)
    hb = h.astype(BF)
    out = None
    for c0 in range(0, w1_ref.shape[1], FF_TILE):
        a = jnp.maximum(jnp.dot(hb, w1_ref[:, c0:c0 + FF_TILE], preferred_element_type=F32), 0.0)
        t = _dot(a * a, w2_ref[c0:c0 + FF_TILE, :])
        out = t if out is None else out + t
    o_ref[...] = x + _mod_rows(ml_ref, mc_ref, 5, is_ctx) * _rms(out, gpost_ref[...])


def _mlp(x2, w1, w2, gpre, gpost, modl, modc, *, seq_len, ctx_len, latent_only):
    m, d = x2.shape
    tm = _row_block(seq_len)
    bps = seq_len // tm
    tok = pl.BlockSpec((tm, d), lambda i: (i, 0))
    kern = functools.partial(_mlp_kernel, blocks_per_seq=None if latent_only else bps, ctx_len=ctx_len)
    return pl.pallas_call(
        kern,
        out_shape=jax.ShapeDtypeStruct((m, d), F32),
        grid=(m // tm,),
        in_specs=[
            tok,
            _resident(w1.shape),
            _resident(w2.shape),
            pl.BlockSpec((1, d), lambda i: (0, 0)),
            pl.BlockSpec((1, d), lambda i: (0, 0)),
            pl.BlockSpec((1, SUBLANES, d), lambda i: (i // bps, 0, 0)),
            pl.BlockSpec((SUBLANES, d), lambda i: (0, 0)),
        ],
        out_specs=tok,
        compiler_params=_cparams(("parallel",)),
        name="mlp",
    )(x2, w1, w2, gpre, gpost, modl, modc)


def _pad_rows(a, rows):
    return jnp.pad(a, ((0, rows - a.shape[0]), (0, 0)))


def _rope_tables(seq, ctx_len):
    rows = seq // GRID_W
    row = jnp.repeat(jnp.arange(rows), GRID_W).astype(F32)
    col = jnp.tile(jnp.arange(GRID_W), rows).astype(F32)
    axis_dim = ATT_HEAD // 2
    freqs = ROPE_THETA ** (-jnp.arange(0, axis_dim, 2, dtype=F32) / axis_dim)
    ang = jnp.concatenate([row[:, None] * freqs, col[:, None] * freqs], axis=-1)
    cos, sin = jnp.cos(ang), jnp.sin(ang)
    cos_t = jnp.concatenate([cos, cos], axis=-1)
    sin_t = jnp.concatenate([-sin, sin], axis=-1)
    cos_t = jnp.concatenate([jnp.ones((ctx_len, ATT_HEAD), F32), cos_t], axis=0)
    sin_t = jnp.concatenate([jnp.zeros((ctx_len, ATT_HEAD), F32), sin_t], axis=0)
    return cos_t, sin_t


def _chunk_tri(upper):
    t = jnp.arange(TOKEN_BLOCK)
    same = (t[:, None] // CHUNK) == (t[None, :] // CHUNK)
    tri = (t[None, :] >= t[:, None]) if upper else (t[None, :] <= t[:, None])
    return jnp.logical_and(same, tri).astype(BF)


def kernel(x, c, ctx, c_ctx, w_mod, b_mod, norm_mix_pre, norm_mix_post, norm_mlp_pre, norm_mlp_post, w_in, rw_mu, rw_w0, rw_w2, rw_a0, rw_a2, rw_g2, rw_k_k, rw_k_a, rw_r_k, rw_ln_g, rw_ln_b, cv_dw_w, cv_dw_b, cv_ln_g, cv_ln_b, at_q_norm, at_k_norm, w_branch, w_out, w_ff1, w_ff2):
    b, s, d = x.shape
    ctx_len = ctx.shape[1]
    depth = w_in.shape[0]
    l = ctx_len + s
    assert d == D_MODEL and ctx_len == TOKEN_BLOCK and s % TOKEN_BLOCK == 0

    mod_rows = -(-(b + 1) // SUBLANES) * SUBLANES
    cvec = _pad_rows(jnp.concatenate([c, c_ctx[None, :]], axis=0), mod_rows)
    mods = _modulation(cvec, w_mod, b_mod).reshape(depth, mod_rows, N_MOD, d)
    mods = jnp.pad(mods, ((0, 0), (0, 0), (0, SUBLANES - N_MOD), (0, 0)))

    cos_t, sin_t = _rope_tables(s, ctx_len)
    half_perm = jnp.concatenate([jnp.arange(0, ATT_HEAD, 2), jnp.arange(1, ATT_HEAD, 2)])
    qk_perm = (jnp.arange(ATT_Q_HEADS + ATT_KV_HEADS)[:, None] * ATT_HEAD + half_perm[None, :]).reshape(-1)
    ones_blk = (jnp.arange(ONES_TILE)[:, None] // RW_HEAD == jnp.arange(ONES_TILE)[None, :] // RW_HEAD).astype(BF)
    trif, trib = _chunk_tri(False), _chunk_tri(True)
    tpos = jnp.arange(TOKEN_BLOCK)
    shift_band = (0.5 * (jnp.abs(tpos[:, None] - tpos[None, :]) == 1) - (tpos[:, None] == tpos[None, :])).astype(BF)
    zpad64 = jnp.zeros((64, d), F32)

    xu = jnp.concatenate([ctx, x], axis=1).reshape(b * l, d)

    for li in range(depth):
        modl, modc = mods[li, :b], mods[li, b]
        w_l = w_in[li]
        c0, c1, c2 = RWKV_COLS, RWKV_COLS + 2 * d, RWKV_COLS + 2 * d + ATT_COLS
        w_rw = w_l[:, :c0].astype(BF)
        w_cv = w_l[:, c0:c1].astype(BF)
        w_at = w_l[:, c1:c2]
        w_at = jnp.concatenate([w_at[:, qk_perm], w_at[:, d + KV_COLS:]], axis=1).astype(BF)
        w_gt = w_l[:, c2:].astype(BF)
        z_rw, z_cv, z_at = _modproj(xu, norm_mix_pre[li][None, :], modl, modc, (w_rw, w_cv, w_at),
                                    (1152, 2048, 1536), seq_len=l, ctx_len=ctx_len, shift_idx=0, scale_idx=1)
        z_rw = z_rw.reshape(b, l, RWKV_COLS)
        z_cv = z_cv.reshape(b, l, 2 * d)
        z_at = z_at.reshape(b, l, ATT_COLS)

        rw_vec = _pad_rows(jnp.stack([rw_w0[li, 0], rw_w0[li, 1], rw_a0[li, 0], rw_a0[li, 1], rw_k_k[li], rw_k_a[li],
                                      rw_r_k[li].reshape(-1)]), SUBLANES)
        w2cat = jnp.concatenate([jnp.concatenate([rw_w2[li, 0], zpad64], axis=0),
                                 jnp.concatenate([zpad64, rw_w2[li, 1]], axis=0)], axis=1).astype(BF)
        a2cat = jnp.concatenate([jnp.concatenate([rw_a2[li, 0], zpad64], axis=0),
                                 jnp.concatenate([zpad64, rw_a2[li, 1]], axis=0)], axis=1).astype(BF)
        v, pf, qf, kf, rf, pb, qb, kb, rb, wc, g, bon = _rwkv_prep(
            z_rw, rw_mu[li][None, :], rw_vec, w2cat, a2cat, rw_g2[li].astype(BF), ones_blk, trif, trib, shift_band)
        yf, yb = _rwkv_scan(v, pf, qf, kf, rf, pb, qb, kb, rb, wc, ctx_len=ctx_len)
        last = li == depth - 1
        first_block = 1 if last else 0
        ln_vec = _pad_rows(jnp.stack([rw_ln_g[li], rw_ln_b[li]]), SUBLANES)
        o_rw = _rwkv_readout(yf.reshape(b * l, d), yb.reshape(b * l, d), bon.reshape(b * l, d), g.reshape(b * l, d),
                             ln_vec, ones_blk, seq_len=l, latent_only=last)

        cv_vec = _pad_rows(jnp.stack([cv_dw_b[li], cv_ln_g[li], cv_ln_b[li]]), SUBLANES)
        dw_w = jnp.broadcast_to(cv_dw_w[li][:, None, :], (CONV_WIDTH, SUBLANES, d))
        o_cv = _conv_branch(z_cv, dw_w, cv_vec, first_block=first_block).reshape(-1, d)

        q, k, vv = _att_prep(z_at, cos_t, sin_t, at_q_norm[li][half_perm][None, :], at_k_norm[li][half_perm][None, :])
        o_at = _attention(q, k, vv, ctx_len=ctx_len, first_block=first_block).reshape(-1, d)

        xu = _merge(o_rw, o_cv, o_at, xu, w_gt, w_branch[li].astype(BF), w_out[li].astype(BF),
                    norm_mix_pre[li][None, :], norm_mix_post[li][None, :], modl, modc,
                    seq_len=l, ctx_len=ctx_len, latent_only=last)
        xu = _mlp(xu, w_ff1[li].astype(BF), w_ff2[li].astype(BF), norm_mlp_pre[li][None, :], norm_mlp_post[li][None, :],
                  modl, modc, seq_len=s if last else l, ctx_len=ctx_len, latent_only=last)

    return xu.reshape(b, s, d)
```

```python
import functools
import math

import jax
import jax.numpy as jnp
from jax import lax
from jax.experimental import pallas as pl
from jax.experimental.pallas import tpu as pltpu

F32 = jnp.float32
BF = jnp.bfloat16

D_MODEL = 1024
N_MOD = 6
EPS = 1e-6
RW_HEAD = 64
RW_HEADS = 16
RW_GN_EPS = 64e-5
RW_LORA_COLS = 384
RWKV_COLS = 3 * D_MODEL + RW_LORA_COLS
CONV_WIDTH = 31
HALO = 16
ONES_TILE = 256
ATT_HEAD = 128
ATT_Q_HEADS = 8
ATT_KV_HEADS = 2
ATT_GROUP = ATT_Q_HEADS // ATT_KV_HEADS
ATT_STACK = 2
KV_COLS = ATT_KV_HEADS * ATT_HEAD
ATT_COLS = D_MODEL + 2 * KV_COLS
GRID_W = 64
ROPE_THETA = 10000.0
D_FF = 4 * D_MODEL

LANES = 128
SUBLANES = 8
CHUNK = 64
TOKEN_BLOCK = 256
PROJ_ROWS = 768
FF_TILE = 1024
VMEM_LIMIT = 56 * 1024 * 1024
EXP_NEG_HALF = math.exp(-0.5)
LOG2_E = math.log2(math.e)


def _cparams(sem):
    return pltpu.CompilerParams(dimension_semantics=sem, vmem_limit_bytes=VMEM_LIMIT)


def _dot(a, b):
    return jnp.dot(a.astype(BF), b.astype(BF), preferred_element_type=F32)


def _dot_nt(a, b):
    return lax.dot_general(a.astype(BF), b.astype(BF), (((1,), (1,)), ((), ())), preferred_element_type=F32)


def _dot_tn(a, b):
    return lax.dot_general(a.astype(BF), b.astype(BF), (((0,), (0,)), ((), ())), preferred_element_type=F32)


def _split_dot(x, w):
    hi = x.astype(BF)
    lo = (x - hi.astype(F32)).astype(BF)
    return jnp.dot(hi, w, preferred_element_type=F32) + jnp.dot(lo, w, preferred_element_type=F32)


def _head_sums(x, ones, split):
    w = ones.shape[0]
    dot = _split_dot if split else _dot
    return jnp.concatenate([dot(x[:, n * w:(n + 1) * w], ones) for n in range(x.shape[1] // w)], axis=1)


def _sigmoid(x):
    return 1.0 / (1.0 + jnp.exp(-x))


def _rms(x, g):
    return x * lax.rsqrt(jnp.mean(x * x, axis=-1, keepdims=True) + EPS) * g


def _mod_kernel(c_ref, w_ref, b_ref, o_ref):
    c = c_ref[...]
    o_ref[0] = _dot(c * _sigmoid(c), w_ref[0]) + b_ref[0]


def _modulation(cvec, w_mod, b_mod):
    depth, d, n = w_mod.shape
    rows = cvec.shape[0]
    tn = 1536
    return pl.pallas_call(
        _mod_kernel,
        out_shape=jax.ShapeDtypeStruct((depth, rows, n), F32),
        grid=(depth, n // tn),
        in_specs=[
            pl.BlockSpec((rows, d), lambda l, j: (0, 0)),
            pl.BlockSpec((1, d, tn), lambda l, j: (l, 0, j)),
            pl.BlockSpec((1, 1, tn), lambda l, j: (l, 0, j)),
        ],
        out_specs=pl.BlockSpec((1, rows, tn), lambda l, j: (l, 0, j)),
        compiler_params=_cparams(("parallel", "parallel")),
        name="modulation",
    )(cvec, w_mod, b_mod.reshape(depth, 1, n))


def _mod_rows(ml_ref, mc_ref, idx, is_ctx):
    return jnp.where(is_ctx, mc_ref[idx:idx + 1, :], ml_ref[0, idx:idx + 1, :])


def _ctx_rows(tm, blocks_per_seq, ctx_len):
    row = (pl.program_id(0) % blocks_per_seq) * tm + lax.broadcasted_iota(jnp.int32, (tm, 1), 0)
    return row < ctx_len


def _row_block(seq_len):
    return max(t for t in range(LANES, PROJ_ROWS + 1, LANES) if seq_len % t == 0)


def _resident(shape):
    return pl.BlockSpec(shape, lambda *_: (0,) * len(shape), pipeline_mode=pl.Buffered(1))


def _modproj_kernel(x_ref, g_ref, ml_ref, mc_ref, *rest, starts, shift_idx, scale_idx, blocks_per_seq, ctx_len):
    ngroups = len(starts) - 1
    w_refs, o_refs, h_ref = rest[:ngroups], rest[ngroups:2 * ngroups], rest[2 * ngroups]
    j = pl.program_id(1)

    @pl.when(j == 0)
    def _():
        xn = _rms(x_ref[...], g_ref[...])
        is_ctx = _ctx_rows(x_ref.shape[0], blocks_per_seq, ctx_len)
        scale = _mod_rows(ml_ref, mc_ref, scale_idx, is_ctx)
        shift = _mod_rows(ml_ref, mc_ref, shift_idx, is_ctx)
        h_ref[...] = (xn * (1.0 + scale) + shift).astype(BF)

    for k in range(ngroups):
        tn = o_refs[k].shape[1]
        for t in range(starts[k + 1] - starts[k]):
            @pl.when(j == starts[k] + t)
            def _(k=k, t=t, tn=tn):
                o_refs[k][...] = jnp.dot(h_ref[...], w_refs[k][:, t * tn:(t + 1) * tn],
                                         preferred_element_type=F32).astype(o_refs[k].dtype)


def _modproj(x2, gain, modl, modc, weights, tiles, *, seq_len, ctx_len, shift_idx, scale_idx):
    m, d = x2.shape
    tm = _row_block(seq_len)
    bps = seq_len // tm
    nblk = [w.shape[1] // tn for w, tn in zip(weights, tiles)]
    starts = [0]
    for nb in nblk:
        starts.append(starts[-1] + nb)

    def col(k):
        return lambda j: jnp.clip(j - starts[k], 0, nblk[k] - 1)

    kern = functools.partial(_modproj_kernel, starts=tuple(starts), shift_idx=shift_idx, scale_idx=scale_idx,
                             blocks_per_seq=bps, ctx_len=ctx_len)
    w_specs = [_resident(w.shape) for w in weights]
    o_specs = [pl.BlockSpec((tm, tn), lambda i, j, c=col(k): (i, c(j))) for k, tn in enumerate(tiles)]
    return pl.pallas_call(
        kern,
        out_shape=[jax.ShapeDtypeStruct((m, w.shape[1]), BF) for w in weights],
        grid=(m // tm, starts[-1]),
        in_specs=[
            pl.BlockSpec((tm, d), lambda i, j: (i, 0)),
            pl.BlockSpec((1, d), lambda i, j: (0, 0)),
            pl.BlockSpec((1, SUBLANES, d), lambda i, j: (i // bps, 0, 0)),
            pl.BlockSpec((SUBLANES, d), lambda i, j: (0, 0)),
        ] + w_specs,
        out_specs=o_specs,
        scratch_shapes=[pltpu.VMEM((tm, d), BF)],
        compiler_params=_cparams(("parallel", "arbitrary")),
        name="modproj",
    )(x2, gain, modl, modc, *weights)


def _rwkv_prep_kernel(z_ref, zp_ref, zn_ref, mu_ref, vec_ref, w2_ref, a2_ref, g2_ref, ones_ref, trif_ref, trib_ref, band_ref,
                      v_ref, pf_ref, qf_ref, kf_ref, rf_ref, pb_ref, qb_ref, kb_ref, rb_ref, wc_ref, g_ref, bon_ref,
                      *, nblocks):
    i = pl.program_id(1)
    zb = z_ref[0]
    z = zb.astype(F32)
    tb = z.shape[0]
    prev_ok = i >= 2
    next_ok = jnp.logical_and(i >= 1, i <= nblocks - 2)
    zp_row = jnp.where(prev_ok, zp_ref[0, HALO - 1:HALO, :].astype(F32), 0.0)
    zn_row = jnp.where(next_ok, zn_ref[0, 0:1, :].astype(F32), 0.0)
    delta = jnp.dot(band_ref[...], zb, preferred_element_type=F32)
    sub = lax.broadcasted_iota(jnp.int32, (SUBLANES, 1), 0)
    top = delta[0:SUBLANES] + jnp.where(sub == 0, 0.5 * zp_row, 0.0)
    bot = delta[tb - SUBLANES:tb] + jnp.where(sub == SUBLANES - 1, 0.5 * zn_row, 0.0)
    delta = jnp.concatenate([top, delta[SUBLANES:tb - SUBLANES], bot], axis=0)
    zs = z + mu_ref[...] * delta

    d = D_MODEL
    r = zs[:, 0:d]
    k = zs[:, d:2 * d]
    v = zs[:, 2 * d:3 * d]
    w_lo = _dot(jnp.tanh(zs[:, 3 * d:3 * d + 128]), w2_ref[...])
    a_lo = _dot(zs[:, 3 * d + 128:3 * d + 256], a2_ref[...])
    g_ref[0] = _dot(_sigmoid(zs[:, 3 * d + 256:3 * d + 384]), g2_ref[...]).astype(BF)

    vec = vec_ref[...]
    ones = ones_ref[...]
    kk0 = k * vec[4:5]
    kk = kk0 * jnp.minimum(lax.rsqrt(_head_sums(kk0 * kk0, ones, True)), 1e12)

    keysum = None
    outs = ((pf_ref, qf_ref, kf_ref, rf_ref, trif_ref), (pb_ref, qb_ref, kb_ref, rb_ref, trib_ref))
    for dr, (p_ref, q_ref, k_ref, r_ref, tri_ref) in enumerate(outs):
        w_raw = vec[dr:dr + 1] + w_lo[:, dr * d:(dr + 1) * d]
        lw = -EXP_NEG_HALF * _sigmoid(w_raw)
        a = _sigmoid(vec[2 + dr:3 + dr] + a_lo[:, dr * d:(dr + 1) * d])
        key = k * (1.0 + (a - 1.0) * vec[5:6])
        cum = _split_dot_left(tri_ref[...], lw)
        wt = jnp.exp(cum)
        iw = jnp.exp(-cum)
        p_ref[0] = (kk * jnp.exp(cum - lw)).astype(BF)
        q_ref[0] = (a * kk * iw).astype(BF)
        k_ref[0] = (key * iw).astype(BF)
        r_ref[0] = (r * wt).astype(BF)
        for c in range(tb // CHUNK):
            last = c * CHUNK + (CHUNK - 1 if dr == 0 else 0)
            wc_ref[0, 0, dr * 4 + c:dr * 4 + c + 1, :] = wt[last:last + 1, :]
        keysum = key if keysum is None else keysum + key

    bon_ref[0] = (_head_sums(r * keysum * vec[6:7], ones, False) * v).astype(BF)
    v_ref[0] = v.astype(BF)


def _split_dot_left(w, x):
    hi = x.astype(BF)
    lo = (x - hi.astype(F32)).astype(BF)
    return jnp.dot(w, hi, preferred_element_type=F32) + jnp.dot(w, lo, preferred_element_type=F32)


def _rwkv_prep(z_rw, mu, vec, w2cat, a2cat, g2, ones, trif, trib, band):
    b, l, cols = z_rw.shape
    tb = TOKEN_BLOCK
    nb = l // tb
    d = D_MODEL
    hb = tb // HALO
    nhalo = l // HALO
    tok = lambda bi, i: (bi, i, 0)
    const2 = lambda bi, i: (0, 0)
    seq_bf = jax.ShapeDtypeStruct((b, l, d), BF)
    out_tok = pl.BlockSpec((1, tb, d), tok)
    kern = functools.partial(_rwkv_prep_kernel, nblocks=nb)
    return pl.pallas_call(
        kern,
        out_shape=[seq_bf] * 9 + [jax.ShapeDtypeStruct((b, nb, SUBLANES, d), F32), seq_bf, seq_bf],
        grid=(b, nb),
        in_specs=[
            pl.BlockSpec((1, tb, cols), tok),
            pl.BlockSpec((1, HALO, cols), lambda bi, i: (bi, jnp.maximum(i * hb - 1, 0), 0)),
            pl.BlockSpec((1, HALO, cols), lambda bi, i: (bi, jnp.minimum((i + 1) * hb, nhalo - 1), 0)),
            pl.BlockSpec((1, cols), const2),
            pl.BlockSpec((SUBLANES, d), const2),
            pl.BlockSpec((LANES, 2 * d), const2),
            pl.BlockSpec((LANES, 2 * d), const2),
            pl.BlockSpec((LANES, d), const2),
            pl.BlockSpec((ONES_TILE, ONES_TILE), const2),
            pl.BlockSpec((tb, tb), const2),
            pl.BlockSpec((tb, tb), const2),
            pl.BlockSpec((tb, tb), const2),
        ],
        out_specs=[out_tok] * 9 + [pl.BlockSpec((1, 1, SUBLANES, d), lambda bi, i: (bi, i, 0, 0)), out_tok, out_tok],
        compiler_params=_cparams(("parallel", "parallel")),
        name="rwkv_prep",
    )(z_rw, z_rw, z_rw, mu, vec, w2cat, a2cat, g2, ones, trif, trib, band)


def _sm(x, m_a):
    return jnp.concatenate([jnp.where(m_a, x, 0), jnp.where(m_a, 0, x)], axis=0).astype(BF)


def _chunk_step(chains, levels, eye, m_a, bd):
    c = CHUNK
    sm = lambda x: _sm(x, m_a)
    pm = lambda x, y: jnp.dot(x.astype(BF), sm(y), preferred_element_type=F32)
    each = lambda fn, *cols: [fn(*args) for args in zip(*cols)]

    strict = [ch["strict"] for ch in chains]
    incl = [ch["incl"] for ch in chains]
    p2, q2, k2, r2, v2, s2 = ([ch[n] for ch in chains] for n in ("p", "q", "k", "r", "v", "s"))
    pr = each(lambda p, r: jnp.concatenate([p, r], axis=0), p2, r2)
    gps = each(lambda a, q, k, s: _dot_nt(a, jnp.concatenate([sm(q), sm(k), s.astype(BF)], axis=0)), pr, q2, k2, s2)
    gram = each(lambda g: g[:, :4 * c], gps)
    prs = each(lambda g: g[:, 4 * c:], gps)
    lm = each(lambda g, m: jnp.where(m, g[:c, :2 * c], 0.0), gram, strict)
    rhs = each(lambda g, m, v, ps: ps[:c] + pm(jnp.where(m, g[:c, 2 * c:], 0.0), v), gram, strict, v2, prs)
    blk8 = functools.reduce(jnp.logical_or, levels[:3])
    dm = each(lambda l: jnp.where(blk8, l, 0.0), lm)
    d2 = each(pm, dm, dm)
    nd = each(lambda d, dd: pm(jnp.concatenate([eye - d, dd], axis=0), dd), dm, d2)
    tinv = each(lambda d, t: ((eye - d) + t[:c]) + pm((eye - d) + t[:c], t[c:]), dm, nd)
    for lvl in levels[3:]:
        tc = each(lambda t, l: pm(t, jnp.where(lvl, l, 0.0)), tinv, lm)
        tinv = each(lambda t, x: t - pm(x, t), tinv, tc)
    u = each(lambda t, r: -pm(t, r), tinv, rhs)
    y = each(lambda g, m, ps, uu, v: ps[c:] + _dot(
        jnp.concatenate([jnp.where(m, g[c:, :2 * c], 0.0), jnp.where(m, g[c:, 2 * c:], 0.0)], axis=1),
        jnp.concatenate([sm(uu), sm(v)], axis=0)), gram, incl, prs, u, v2)
    upd = each(lambda uu, v, q, k: _dot_tn(jnp.concatenate([uu.astype(BF), v], axis=0), jnp.concatenate([q, k], axis=0)),
               u, v2, q2, k2)
    s_new = each(lambda s, up, ch: jnp.where(bd, s + up, 0.0) * ch["w"], s2, upd, chains)
    return y, s_new


def _rwkv_scan_kernel(pf_ref, qf_ref, kf_ref, rf_ref, vf_ref, wcf_ref, pb_ref, qb_ref, kb_ref, rb_ref, vb_ref, wcb_ref,
                      yf_ref, yb_ref, sf_ref, sb_ref, *, bwd_chunk):
    n = pl.program_id(1)
    c = CHUNK
    npairs = sf_ref.shape[0]

    @pl.when(n == 0)
    def _():
        sf_ref[...] = jnp.zeros_like(sf_ref)
        sb_ref[...] = jnp.zeros_like(sb_ref)

    t_i = lax.broadcasted_iota(jnp.int32, (c, 2 * c), 0)
    s_i = jnp.bitwise_and(lax.broadcasted_iota(jnp.int32, (c, 2 * c), 1), c - 1)
    m_a = lax.broadcasted_iota(jnp.int32, (1, LANES), 1) < RW_HEAD
    levels = [jnp.logical_and(t_i // (2 * hb) == s_i // (2 * hb), t_i // hb != s_i // hb)
              for hb in (2 ** k for k in range(c.bit_length() - 1))]
    eye = (t_i == s_i).astype(F32)
    bd = (lax.broadcasted_iota(jnp.int32, (LANES, LANES), 0) // RW_HEAD) == \
         (lax.broadcasted_iota(jnp.int32, (LANES, LANES), 1) // RW_HEAD)
    per_blk = TOKEN_BLOCK // c
    wrow_f = wcf_ref[0, 0, pl.ds(n % per_blk, 1), :]
    wrow_b = wcb_ref[0, 0, pl.ds(per_blk + bwd_chunk(n) % per_blk, 1), :]

    dirs = (
        (pf_ref, qf_ref, kf_ref, rf_ref, vf_ref, wrow_f, yf_ref, sf_ref, s_i < t_i, s_i <= t_i),
        (pb_ref, qb_ref, kb_ref, rb_ref, vb_ref, wrow_b, yb_ref, sb_ref, s_i > t_i, s_i >= t_i),
    )
    chains, sinks = [], []
    for p_ref, q_ref, k_ref, r_ref, v_ref, wrow, y_ref, s_ref, strict, incl in dirs:
        for p in range(npairs):
            sl = slice(p * LANES, (p + 1) * LANES)
            chains.append(dict(p=p_ref[0, :, sl], q=q_ref[0, :, sl], k=k_ref[0, :, sl], r=r_ref[0, :, sl],
                               v=v_ref[0, :, sl], s=s_ref[p], w=wrow[:, sl], strict=strict, incl=incl))
            sinks.append((y_ref, s_ref, p, sl))
    ys, s_news = _chunk_step(chains, levels, eye, m_a, bd)
    for (y_ref, s_ref, p, sl), y, s_new in zip(sinks, ys, s_news):
        y_ref[0, :, sl] = y.astype(y_ref.dtype)
        s_ref[p] = s_new


def _rwkv_scan(v, pf, qf, kf, rf, pb, qb, kb, rb, wc, *, ctx_len):
    b, l, d = v.shape
    c = CHUNK
    nch = l // c
    nctx = ctx_len // c
    per_blk = TOKEN_BLOCK // c

    def bwd_chunk(n):
        return jnp.where(n < nctx, nctx - 1 - n, nch + nctx - 1 - n)

    fwd = pl.BlockSpec((1, c, d), lambda bi, n: (bi, n, 0))
    bwd = pl.BlockSpec((1, c, d), lambda bi, n: (bi, bwd_chunk(n), 0))
    wcf = pl.BlockSpec((1, 1, SUBLANES, d), lambda bi, n: (bi, n // per_blk, 0, 0))
    wcb = pl.BlockSpec((1, 1, SUBLANES, d), lambda bi, n: (bi, bwd_chunk(n) // per_blk, 0, 0))
    npairs = d // LANES
    kern = functools.partial(_rwkv_scan_kernel, bwd_chunk=bwd_chunk)
    return pl.pallas_call(
        kern,
        out_shape=[jax.ShapeDtypeStruct((b, l, d), BF)] * 2,
        grid=(b, nch),
        in_specs=[fwd] * 5 + [wcf] + [bwd] * 5 + [wcb],
        out_specs=[fwd, bwd],
        scratch_shapes=[pltpu.VMEM((npairs, LANES, LANES), F32)] * 2,
        compiler_params=_cparams(("parallel", "arbitrary")),
        name="rwkv_scan",
    )(pf, qf, kf, rf, v, wc, pb, qb, kb, rb, v, wc)


def _rwkv_readout_kernel(yf_ref, yb_ref, bon_ref, g_ref, vec_ref, ones_ref, o_ref):
    ones = ones_ref[...]
    y = yf_ref[...].astype(F32) + yb_ref[...].astype(F32)
    inv_n = 1.0 / RW_HEAD
    yc = y - _head_sums(y, ones, True) * inv_n
    var = _head_sums(yc * yc, ones, True) * inv_n
    yn = yc * lax.rsqrt(var + RW_GN_EPS) * vec_ref[0:1, :] + vec_ref[1:2, :]
    o_ref[...] = ((yn + bon_ref[...].astype(F32)) * g_ref[...].astype(F32)).astype(BF)


def _latent_block(nblocks):
    return lambda i: (i // (nblocks - 1)) * nblocks + 1 + i % (nblocks - 1)


def _rwkv_readout(yf, yb, bon, g, vec, ones, *, seq_len, latent_only):
    m, d = yf.shape
    tm = TOKEN_BLOCK
    nb = seq_len // tm
    tok = pl.BlockSpec((tm, d), lambda i: (i, 0))
    src = pl.BlockSpec((tm, d), lambda i, f=_latent_block(nb): (f(i), 0)) if latent_only else tok
    rows = m // nb * (nb - 1) if latent_only else m
    return pl.pallas_call(
        _rwkv_readout_kernel,
        out_shape=jax.ShapeDtypeStruct((rows, d), BF),
        grid=(rows // tm,),
        in_specs=[src, src, src, src, pl.BlockSpec((SUBLANES, d), lambda i: (0, 0)),
                  pl.BlockSpec((ONES_TILE, ONES_TILE), lambda i: (0, 0))],
        out_specs=tok,
        compiler_params=_cparams(("parallel",)),
        name="rwkv_readout",
    )(yf, yb, bon, g, vec, ones)


CONV_ROWS = 32
CONV_LANES = 512


def _conv_kernel(z_ref, zp_ref, zn_ref, w_ref, vec_ref, o_ref, gp_ref, gs_ref, u_ref, *, nblocks, first_block):
    i = pl.program_id(1) + first_block
    tb = z_ref.shape[1]
    d = D_MODEL
    h = HALO

    def glu(zz):
        zz = zz.astype(F32)
        return zz[:, :d] * _sigmoid(zz[:, d:])

    prev_ok = i >= 2
    next_ok = jnp.logical_and(i >= 1, i <= nblocks - 2)
    gp_ref[0:h, :] = jnp.where(prev_ok, glu(zp_ref[0]), 0.0)
    gp_ref[h:h + tb, :] = glu(z_ref[0])
    gp_ref[h + tb:2 * h + tb, :] = jnp.where(next_ok, glu(zn_ref[0]), 0.0)
    rows = gs_ref.shape[1]
    for s in range(SUBLANES):
        gs_ref[s] = gp_ref[s:s + rows, :]

    off = h - CONV_WIDTH // 2
    grp = CONV_ROWS // SUBLANES
    for r0 in range(0, tb, CONV_ROWS):
        for c0 in range(0, d, CONV_LANES):
            acc = jnp.zeros((grp, SUBLANES, CONV_LANES), F32)
            for k in range(CONV_WIDTH):
                a, s = divmod(k + off, SUBLANES)
                win = gs_ref[s, r0 + a * SUBLANES:r0 + a * SUBLANES + CONV_ROWS, c0:c0 + CONV_LANES]
                acc = acc + win.reshape(grp, SUBLANES, CONV_LANES) * w_ref[k, :, c0:c0 + CONV_LANES][None]
            u_ref[r0:r0 + CONV_ROWS, c0:c0 + CONV_LANES] = acc.reshape(CONV_ROWS, CONV_LANES)

    u = u_ref[...] + vec_ref[0:1, :]
    mu = jnp.mean(u, axis=-1, keepdims=True)
    uc = u - mu
    var = jnp.mean(uc * uc, axis=-1, keepdims=True)
    y = uc * lax.rsqrt(var + EPS) * vec_ref[1:2, :] + vec_ref[2:3, :]
    o_ref[0] = (y * _sigmoid(y)).astype(BF)


def _conv_branch(z_cv, w, vec, *, first_block):
    b, l, cols = z_cv.shape
    tb = TOKEN_BLOCK
    nb = l // tb
    d = D_MODEL
    hb = tb // HALO
    nhalo = l // HALO
    f0 = first_block
    shifted_rows = tb + 2 * HALO - SUBLANES
    kern = functools.partial(_conv_kernel, nblocks=nb, first_block=f0)
    return pl.pallas_call(
        kern,
        out_shape=jax.ShapeDtypeStruct((b, l - f0 * tb, d), BF),
        grid=(b, nb - f0),
        in_specs=[
            pl.BlockSpec((1, tb, cols), lambda bi, i: (bi, i + f0, 0)),
            pl.BlockSpec((1, HALO, cols), lambda bi, i: (bi, jnp.maximum((i + f0) * hb - 1, 0), 0)),
            pl.BlockSpec((1, HALO, cols), lambda bi, i: (bi, jnp.minimum((i + f0 + 1) * hb, nhalo - 1), 0)),
            pl.BlockSpec((CONV_WIDTH, SUBLANES, d), lambda bi, i: (0, 0, 0)),
            pl.BlockSpec((SUBLANES, d), lambda bi, i: (0, 0)),
        ],
        out_specs=pl.BlockSpec((1, tb, d), lambda bi, i: (bi, i, 0)),
        scratch_shapes=[pltpu.VMEM((tb + 2 * HALO, d), F32), pltpu.VMEM((SUBLANES, shifted_rows, d), F32),
                        pltpu.VMEM((tb, d), F32)],
        compiler_params=_cparams(("parallel", "parallel")),
        name="conv_branch",
    )(z_cv, z_cv, z_cv, w, vec)


def _norm_rope(x, g, cos, sin):
    xn = _rms(x.astype(F32), g)
    return xn * cos + pltpu.roll(xn, ATT_HEAD // 2, 1) * sin


def _att_prep_kernel(z_ref, cos_ref, sin_ref, gq_ref, gk_ref, q_ref, k_ref, v_ref):
    hd = ATT_HEAD
    cos = cos_ref[...]
    sin = sin_ref[...]
    q_scale = (hd ** -0.5) * LOG2_E
    for h in range(ATT_Q_HEADS):
        sl = slice(h * hd, (h + 1) * hd)
        q_ref[0, :, sl] = (_norm_rope(z_ref[0, :, sl], gq_ref[...], cos, sin) * q_scale).astype(BF)
    for h in range(ATT_KV_HEADS):
        sl = slice(h * hd, (h + 1) * hd)
        k_ref[0, :, sl] = _norm_rope(z_ref[0, :, D_MODEL + h * hd:D_MODEL + (h + 1) * hd], gk_ref[...], cos, sin).astype(BF)
        v_ref[0, :, 2 * h * hd:(2 * h + 1) * hd] = z_ref[0, :, D_MODEL + KV_COLS + h * hd:D_MODEL + KV_COLS + (h + 1) * hd]
        v_ref[0, :, (2 * h + 1) * hd:(2 * h + 2) * hd] = jnp.ones((z_ref.shape[1], hd), BF)


def _att_prep(z_at, cos_t, sin_t, gq, gk):
    b, l, cols = z_at.shape
    tb = TOKEN_BLOCK
    tok = lambda bi, i: (bi, i, 0)
    return pl.pallas_call(
        _att_prep_kernel,
        out_shape=[jax.ShapeDtypeStruct((b, l, D_MODEL), BF), jax.ShapeDtypeStruct((b, l, KV_COLS), BF),
                   jax.ShapeDtypeStruct((b, l, 2 * KV_COLS), BF)],
        grid=(b, l // tb),
        in_specs=[
            pl.BlockSpec((1, tb, cols), tok),
            pl.BlockSpec((tb, ATT_HEAD), lambda bi, i: (i, 0)),
            pl.BlockSpec((tb, ATT_HEAD), lambda bi, i: (i, 0)),
            pl.BlockSpec((1, ATT_HEAD), lambda bi, i: (0, 0)),
            pl.BlockSpec((1, ATT_HEAD), lambda bi, i: (0, 0)),
        ],
        out_specs=[pl.BlockSpec((1, tb, D_MODEL), tok), pl.BlockSpec((1, tb, KV_COLS), tok),
                   pl.BlockSpec((1, tb, 2 * KV_COLS), tok)],
        compiler_params=_cparams(("parallel", "parallel")),
        name="att_prep",
    )(z_at, cos_t, sin_t, gq, gk)


def _attn_kernel(q_ref, k_ref, v_ref, o_ref, *, ctx_len, first_block):
    hd = ATT_HEAD

    def run(nk):
        s_prev = p_prev = None
        tq = q_ref.shape[1]
        nunits = ATT_Q_HEADS // ATT_STACK
        for un in range(nunits + 2):
            s_new = None
            if un < nunits:
                hk = un * ATT_STACK // ATT_GROUP
                qs = [q_ref[0, :, h * hd:(h + 1) * hd] for h in range(un * ATT_STACK, (un + 1) * ATT_STACK)]
                s_new = _dot_nt(qs[0] if ATT_STACK == 1 else jnp.concatenate(qs, axis=0), k_ref[0, 0:nk, hk * hd:(hk + 1) * hd])
            p_new = None
            if s_prev is not None:
                p_new = jnp.exp2(s_prev - jnp.max(s_prev, axis=-1, keepdims=True)).astype(BF)
            if p_prev is not None:
                uo = un - 2
                hk = uo * ATT_STACK // ATT_GROUP
                o = jnp.dot(p_prev, v_ref[0, 0:nk, 2 * hk * hd:(2 * hk + 2) * hd], preferred_element_type=F32)
                o = (o[:, :hd] / o[:, hd:]).astype(BF)
                for g in range(ATT_STACK):
                    ho = uo * ATT_STACK + g
                    o_ref[0, :, ho * hd:(ho + 1) * hd] = o[g * tq:(g + 1) * tq]
            s_prev, p_prev = s_new, p_new

    if first_block == 0:
        @pl.when(pl.program_id(1) == 0)
        def _():
            run(ctx_len)

        @pl.when(pl.program_id(1) > 0)
        def _():
            run(k_ref.shape[1])
    else:
        run(k_ref.shape[1])


def _attention(q, k, v, *, ctx_len, first_block):
    b, l, d = q.shape
    tb = TOKEN_BLOCK
    f0 = first_block
    kern = functools.partial(_attn_kernel, ctx_len=ctx_len, first_block=f0)
    seq = lambda bi, i: (bi, 0, 0)
    return pl.pallas_call(
        kern,
        out_shape=jax.ShapeDtypeStruct((b, l - f0 * tb, d), BF),
        grid=(b, l // tb - f0),
        in_specs=[
            pl.BlockSpec((1, tb, d), lambda bi, i: (bi, i + f0, 0)),
            pl.BlockSpec((1, l, KV_COLS), seq),
            pl.BlockSpec((1, l, 2 * KV_COLS), seq),
        ],
        out_specs=pl.BlockSpec((1, tb, d), lambda bi, i: (bi, i, 0)),
        compiler_params=_cparams(("parallel", "parallel")),
        name="attention",
    )(q, k, v)


def _merge_kernel(orw_ref, ocv_ref, oat_ref, x_ref, wg_ref, wb_ref, wo_ref, gpre_ref, g_ref, ml_ref, mc_ref, o_ref,
                  *, blocks_per_seq, ctx_len):
    d = D_MODEL
    is_ctx = False if blocks_per_seq is None else _ctx_rows(x_ref.shape[0], blocks_per_seq, ctx_len)
    x = x_ref[...]
    h = (_rms(x, gpre_ref[...]) * (1.0 + _mod_rows(ml_ref, mc_ref, 1, is_ctx)) + _mod_rows(ml_ref, mc_ref, 0, is_ctx)).astype(BF)
    m = None
    for n, o_n in enumerate((orw_ref, ocv_ref, oat_ref)):
        gate = _sigmoid(jnp.dot(h, wg_ref[:, n * d:(n + 1) * d], preferred_element_type=F32))
        t = gate * jnp.dot(o_n[...], wb_ref[n], preferred_element_type=F32)
        m = t if m is None else m + t
    out = _dot(m, wo_ref[...])
    o_ref[...] = x + _mod_rows(ml_ref, mc_ref, 2, is_ctx) * _rms(out, g_ref[...])


def _merge(orw, ocv, oat, x2, wg, wb, wo, gain_pre, gain, modl, modc, *, seq_len, ctx_len, latent_only):
    m, d = x2.shape
    if latent_only:
        tm = TOKEN_BLOCK
        nb = seq_len // tm
        bps = nb - 1
        rows = m // nb * bps
        src = _latent_block(nb)
        kern = functools.partial(_merge_kernel, blocks_per_seq=None, ctx_len=ctx_len)
    else:
        tm = _row_block(seq_len)
        bps = seq_len // tm
        rows = m
        src = lambda i: i
        kern = functools.partial(_merge_kernel, blocks_per_seq=bps, ctx_len=ctx_len)
    tok = pl.BlockSpec((tm, d), lambda i: (i, 0))
    return pl.pallas_call(
        kern,
        out_shape=jax.ShapeDtypeStruct((rows, d), F32),
        grid=(rows // tm,),
        in_specs=[
            tok, tok, tok,
            pl.BlockSpec((tm, d), lambda i: (src(i), 0)),
            _resident(wg.shape),
            _resident(wb.shape),
            _resident(wo.shape),
            pl.BlockSpec((1, d), lambda i: (0, 0)),
            pl.BlockSpec((1, d), lambda i: (0, 0)),
            pl.BlockSpec((1, SUBLANES, d), lambda i: (i // bps, 0, 0)),
            pl.BlockSpec((SUBLANES, d), lambda i: (0, 0)),
        ],
        out_specs=tok,
        compiler_params=_cparams(("parallel",)),
        name="merge",
    )(orw, ocv, oat, x2, wg, wb, wo, gain_pre, gain, modl, modc)


def _mlp_kernel(x_ref, w1_ref, w2_ref, gpre_ref, gpost_ref, ml_ref, mc_ref, o_ref, *, blocks_per_seq, ctx_len):
    is_ctx = False if blocks_per_seq is None else _ctx_rows(x_ref.shape[0], blocks_per_seq, ctx_len)
    x = x_ref[...]
    h = _rms(x, gpre_ref[...]) * (1.0 + _mod_rows(ml_ref, mc_ref, 4, is_ctx)) + _mod_rows(ml_ref, mc_ref, 3, is_ctx)
    hb = h.astype(BF)
    out = None
    for c0 in range(0, w1_ref.shape[1], FF_TILE):
        a = jnp.maximum(jnp.dot(hb, w1_ref[:, c0:c0 + FF_TILE], preferred_element_type=F32), 0.0)
        t = _dot(a * a, w2_ref[c0:c0 + FF_TILE, :])
        out = t if out is None else out + t
    o_ref[...] = x + _mod_rows(ml_ref, mc_ref, 5, is_ctx) * _rms(out, gpost_ref[...])


def _mlp(x2, w1, w2, gpre, gpost, modl, modc, *, seq_len, ctx_len, latent_only):
    m, d = x2.shape
    tm = _row_block(seq_len)
    bps = seq_len // tm
    tok = pl.BlockSpec((tm, d), lambda i: (i, 0))
    kern = functools.partial(_mlp_kernel, blocks_per_seq=None if latent_only else bps, ctx_len=ctx_len)
    return pl.pallas_call(
        kern,
        out_shape=jax.ShapeDtypeStruct((m, d), F32),
        grid=(m // tm,),
        in_specs=[
            tok,
            _resident(w1.shape),
            _resident(w2.shape),
            pl.BlockSpec((1, d), lambda i: (0, 0)),
            pl.BlockSpec((1, d), lambda i: (0, 0)),
            pl.BlockSpec((1, SUBLANES, d), lambda i: (i // bps, 0, 0)),
            pl.BlockSpec((SUBLANES, d), lambda i: (0, 0)),
        ],
        out_specs=tok,
        compiler_params=_cparams(("parallel",)),
        name="mlp",
    )(x2, w1, w2, gpre, gpost, modl, modc)


def _pad_rows(a, rows):
    return jnp.pad(a, ((0, rows - a.shape[0]), (0, 0)))


def _rope_tables(seq, ctx_len):
    rows = seq // GRID_W
    row = jnp.repeat(jnp.arange(rows), GRID_W).astype(F32)
    col = jnp.tile(jnp.arange(GRID_W), rows).astype(F32)
    axis_dim = ATT_HEAD // 2
    freqs = ROPE_THETA ** (-jnp.arange(0, axis_dim, 2, dtype=F32) / axis_dim)
    ang = jnp.concatenate([row[:, None] * freqs, col[:, None] * freqs], axis=-1)
    cos, sin = jnp.cos(ang), jnp.sin(ang)
    cos_t = jnp.concatenate([cos, cos], axis=-1)
    sin_t = jnp.concatenate([-sin, sin], axis=-1)
    cos_t = jnp.concatenate([jnp.ones((ctx_len, ATT_HEAD), F32), cos_t], axis=0)
    sin_t = jnp.concatenate([jnp.zeros((ctx_len, ATT_HEAD), F32), sin_t], axis=0)
    return cos_t, sin_t


def _chunk_tri(upper):
    t = jnp.arange(TOKEN_BLOCK)
    same = (t[:, None] // CHUNK) == (t[None, :] // CHUNK)
    tri = (t[None, :] >= t[:, None]) if upper else (t[None, :] <= t[:, None])
    return jnp.logical_and(same, tri).astype(BF)


def kernel(x, c, ctx, c_ctx, w_mod, b_mod, norm_mix_pre, norm_mix_post, norm_mlp_pre, norm_mlp_post, w_in, rw_mu, rw_w0, rw_w2, rw_a0, rw_a2, rw_g2, rw_k_k, rw_k_a, rw_r_k, rw_ln_g, rw_ln_b, cv_dw_w, cv_dw_b, cv_ln_g, cv_ln_b, at_q_norm, at_k_norm, w_branch, w_out, w_ff1, w_ff2):
    b, s, d = x.shape
    ctx_len = ctx.shape[1]
    depth = w_in.shape[0]
    l = ctx_len + s
    assert d == D_MODEL and ctx_len == TOKEN_BLOCK and s % TOKEN_BLOCK == 0

    mod_rows = -(-(b + 1) // SUBLANES) * SUBLANES
    cvec = _pad_rows(jnp.concatenate([c, c_ctx[None, :]], axis=0), mod_rows)
    mods = _modulation(cvec, w_mod, b_mod).reshape(depth, mod_rows, N_MOD, d)
    mods = jnp.pad(mods, ((0, 0), (0, 0), (0, SUBLANES - N_MOD), (0, 0)))

    cos_t, sin_t = _rope_tables(s, ctx_len)
    half_perm = jnp.concatenate([jnp.arange(0, ATT_HEAD, 2), jnp.arange(1, ATT_HEAD, 2)])
    qk_perm = (jnp.arange(ATT_Q_HEADS + ATT_KV_HEADS)[:, None] * ATT_HEAD + half_perm[None, :]).reshape(-1)
    ones_blk = (jnp.arange(ONES_TILE)[:, None] // RW_HEAD == jnp.arange(ONES_TILE)[None, :] // RW_HEAD).astype(BF)
    trif, trib = _chunk_tri(False), _chunk_tri(True)
    tpos = jnp.arange(TOKEN_BLOCK)
    shift_band = (0.5 * (jnp.abs(tpos[:, None] - tpos[None, :]) == 1) - (tpos[:, None] == tpos[None, :])).astype(BF)
    zpad64 = jnp.zeros((64, d), F32)

    xu = jnp.concatenate([ctx, x], axis=1).reshape(b * l, d)

    for li in range(depth):
        modl, modc = mods[li, :b], mods[li, b]
        w_l = w_in[li]
        c0, c1, c2 = RWKV_COLS, RWKV_COLS + 2 * d, RWKV_COLS + 2 * d + ATT_COLS
        w_rw = w_l[:, :c0].astype(BF)
        w_cv = w_l[:, c0:c1].astype(BF)
        w_at = w_l[:, c1:c2]
        w_at = jnp.concatenate([w_at[:, qk_perm], w_at[:, d + KV_COLS:]], axis=1).astype(BF)
        w_gt = w_l[:, c2:].astype(BF)
        z_rw, z_cv, z_at = _modproj(xu, norm_mix_pre[li][None, :], modl, modc, (w_rw, w_cv, w_at),
                                    (1152, 2048, 1536), seq_len=l, ctx_len=ctx_len, shift_idx=0, scale_idx=1)
        z_rw = z_rw.reshape(b, l, RWKV_COLS)
        z_cv = z_cv.reshape(b, l, 2 * d)
        z_at = z_at.reshape(b, l, ATT_COLS)

        rw_vec = _pad_rows(jnp.stack([rw_w0[li, 0], rw_w0[li, 1], rw_a0[li, 0], rw_a0[li, 1], rw_k_k[li], rw_k_a[li],
                                      rw_r_k[li].reshape(-1)]), SUBLANES)
        w2cat = jnp.concatenate([jnp.concatenate([rw_w2[li, 0], zpad64], axis=0),
                                 jnp.concatenate([zpad64, rw_w2[li, 1]], axis=0)], axis=1).astype(BF)
        a2cat = jnp.concatenate([jnp.concatenate([rw_a2[li, 0], zpad64], axis=0),
                                 jnp.concatenate([zpad64, rw_a2[li, 1]], axis=0)], axis=1).astype(BF)
        v, pf, qf, kf, rf, pb, qb, kb, rb, wc, g, bon = _rwkv_prep(
            z_rw, rw_mu[li][None, :], rw_vec, w2cat, a2cat, rw_g2[li].astype(BF), ones_blk, trif, trib, shift_band)
        yf, yb = _rwkv_scan(v, pf, qf, kf, rf, pb, qb, kb, rb, wc, ctx_len=ctx_len)
        last = li == depth - 1
        first_block = 1 if last else 0
        ln_vec = _pad_rows(jnp.stack([rw_ln_g[li], rw_ln_b[li]]), SUBLANES)
        o_rw = _rwkv_readout(yf.reshape(b * l, d), yb.reshape(b * l, d), bon.reshape(b * l, d), g.reshape(b * l, d),
                             ln_vec, ones_blk, seq_len=l, latent_only=last)

        cv_vec = _pad_rows(jnp.stack([cv_dw_b[li], cv_ln_g[li], cv_ln_b[li]]), SUBLANES)
        dw_w = jnp.broadcast_to(cv_dw_w[li][:, None, :], (CONV_WIDTH, SUBLANES, d))
        o_cv = _conv_branch(z_cv, dw_w, cv_vec, first_block=first_block).reshape(-1, d)

        q, k, vv = _att_prep(z_at, cos_t, sin_t, at_q_norm[li][half_perm][None, :], at_k_norm[li][half_perm][None, :])
        o_at = _attention(q, k, vv, ctx_len=ctx_len, first_block=first_block).reshape(-1, d)

        xu = _merge(o_rw, o_cv, o_at, xu, w_gt, w_branch[li].astype(BF), w_out[li].astype(BF),
                    norm_mix_pre[li][None, :], norm_mix_post[li][None, :], modl, modc,
                    seq_len=l, ctx_len=ctx_len, latent_only=last)
        xu = _mlp(xu, w_ff1[li].astype(BF), w_ff2[li].astype(BF), norm_mlp_pre[li][None, :], norm_mlp_post[li][None, :],
                  modl, modc, seq_len=s if last else l, ctx_len=ctx_len, latent_only=last)

    return xu.reshape(b, s, d)
```
